```python
import math, functools
import jax, jax.numpy as jnp
from jax import lax
import numpy as np


D_MODEL = 2048
BATCH = 8
SEQ = 8192
DEPTH = 4

GRID_W = 64
CTX_LEN = 256
N_MIXERS = 2
N_MOD = 6
MLA_HEADS = 16
MLA_Q_RANK = 512
MLA_KV_RANK = 512
MLA_NOPE = 128
MLA_ROPE = 64
MLA_V = 128
GQA_HEADS = 16
GQA_KV_HEADS = 4
GQA_HEAD_DIM = 128
D_FF = 5632
CONV_W = 3

ROPE_BASE = 10000.0
EPS = 1e-6
Q_BLOCK = 128
N_MLA_LAYERS = (DEPTH + N_MIXERS - 1) // N_MIXERS
N_GQA_LAYERS = DEPTH // N_MIXERS
MLA_SCALE = 1.0 / math.sqrt(MLA_NOPE + MLA_ROPE)
GQA_SCALE = 1.0 / math.sqrt(GQA_HEAD_DIM)

kernel_name = "hybrid_mla_gqa_convffn_dit"


def rms_norm(x, g):
    xf = x.astype(jnp.float32)
    y = xf * lax.rsqrt(jnp.mean(xf * xf, axis=-1, keepdims=True) + EPS)
    return (y * g.astype(jnp.float32)).astype(x.dtype)


def modulate(x, g, shift, scale):
    return rms_norm(x, g) * (1.0 + scale) + shift


def axial_rope_tables(rows, cols, rot_dim):
    axis_dim = rot_dim // 2
    inv = jnp.power(ROPE_BASE, -jnp.arange(0, axis_dim, 2, dtype=jnp.float32) / axis_dim)
    ang_r = rows.astype(jnp.float32)[:, None] * inv
    ang_c = cols.astype(jnp.float32)[:, None] * inv
    ang = jnp.concatenate([ang_r, ang_r, ang_c, ang_c], axis=-1)
    return jnp.cos(ang), jnp.sin(ang)


def rotate_half(x):
    x1, x2 = jnp.split(x, 2, axis=-1)
    return jnp.concatenate([-x2, x1], axis=-1)


def apply_axial_rope(x, cos, sin):
    half = x.shape[-1] // 2
    rot = jnp.concatenate([rotate_half(x[..., :half]), rotate_half(x[..., half:])], axis=-1)
    return (x * cos[:, None, :] + rot * sin[:, None, :]).astype(x.dtype)


def attend(q, k, v, scale):
    B, Q, H, Dq = q.shape
    Hk = k.shape[2]
    qg = q.reshape(B, Q, Hk, H // Hk, Dq)
    s = jnp.einsum("bqkgd,btkd->bkgqt", qg, k, preferred_element_type=jnp.float32) * scale
    p = jax.nn.softmax(s, axis=-1)
    o = jnp.einsum("bkgqt,btkd->bqkgd", p.astype(v.dtype), v, preferred_element_type=jnp.float32)
    return o.reshape(B, Q, H, v.shape[-1]).astype(q.dtype)


def blocked_attention(q, k, v, scale):
    B, S, H, Dq = q.shape
    nb = S // Q_BLOCK
    qs = q.reshape(B, nb, Q_BLOCK, H, Dq).swapaxes(0, 1)
    o = lax.map(lambda qb: attend(qb, k, v, scale), qs)
    return o.swapaxes(0, 1).reshape(B, S, H, v.shape[-1])


def mla_queries(h, rope, w_dq, g_dq, w_uq, g_q_nope, g_q_pe):
    B, S, _ = h.shape
    cq = rms_norm(h @ w_dq, g_dq)
    q = (cq @ w_uq).reshape(B, S, MLA_HEADS, MLA_NOPE + MLA_ROPE)
    q_nope = rms_norm(q[..., :MLA_NOPE], g_q_nope)
    q_pe = rms_norm(q[..., MLA_NOPE:], g_q_pe)
    if rope is not None:
        q_pe = apply_axial_rope(q_pe, *rope)
    return jnp.concatenate([q_nope, q_pe], axis=-1)


def mla_keys_values(h, rope, w_dkv, g_dkv, g_k_pe, w_ukv, g_k_nope):
    B, S, _ = h.shape
    kv_a = h @ w_dkv
    c_kv = rms_norm(kv_a[..., :MLA_KV_RANK], g_dkv)
    k_pe = rms_norm(kv_a[..., MLA_KV_RANK:], g_k_pe)[:, :, None, :]
    if rope is not None:
        k_pe = apply_axial_rope(k_pe, *rope)
    kv = (c_kv @ w_ukv).reshape(B, S, MLA_HEADS, MLA_NOPE + MLA_V)
    k_nope = rms_norm(kv[..., :MLA_NOPE], g_k_nope)
    v = kv[..., MLA_NOPE:]
    k = jnp.concatenate([k_nope, jnp.broadcast_to(k_pe, (B, S, MLA_HEADS, MLA_ROPE))], axis=-1)
    return k, v


def gqa_queries(h, rope, w_q, g_q):
    B, S, _ = h.shape
    q = rms_norm((h @ w_q).reshape(B, S, GQA_HEADS, GQA_HEAD_DIM), g_q)
    if rope is not None:
        q = apply_axial_rope(q, *rope)
    return q


def gqa_keys_values(h, rope, w_kv, g_k):
    B, S, _ = h.shape
    kv = (h @ w_kv).reshape(B, S, 2, GQA_KV_HEADS, GQA_HEAD_DIM)
    k = rms_norm(kv[:, :, 0], g_k)
    v = kv[:, :, 1]
    if rope is not None:
        k = apply_axial_rope(k, *rope)
    return k, v


def depthwise_conv_centred(u, w, b):
    S = u.shape[1]
    pad = CONV_W // 2
    up = jnp.pad(u, ((0, 0), (pad, pad), (0, 0)))
    return sum(up[:, k:k + S] * w[k] for k in range(CONV_W)) + b


def conv_ffn(h, w_up, conv_w, conv_b, w_down):
    u = h @ w_up
    gate, val = u[..., :D_FF], u[..., D_FF:]
    gate = depthwise_conv_centred(gate, conv_w, conv_b)
    return (jax.nn.silu(gate) * val) @ w_down


def _fwd_setup_inputs(seed: int = 0) -> dict:
    key = jax.random.key(seed)
    ks = iter(jax.random.split(key, 40))
    D, L, LA, LB = D_MODEL, DEPTH, N_MLA_LAYERS, N_GQA_LAYERS

    def nrm(shape, scale):
        return jax.random.normal(next(ks), shape, jnp.float32) * scale

    def gain(shape):
        return 1.0 + nrm(shape, 0.02)

    return {
        "x": nrm((BATCH, SEQ, D), 1.0),
        "c": nrm((BATCH, D), 1.0),
        "ctx": nrm((BATCH, CTX_LEN, D), 1.0),
        "c_ctx": nrm((D,), 1.0),
        "w_mod": nrm((L, D, N_MOD * D), 0.5 * D ** -0.5),
        "b_mod": nrm((L, N_MOD * D), 0.01),
        "norm_mix": gain((L, D)),
        "norm_ffn": gain((L, D)),
        "mla_w_dq": nrm((LA, D, MLA_Q_RANK), D ** -0.5),
        "mla_g_dq": gain((LA, MLA_Q_RANK)),
        "mla_w_uq": nrm((LA, MLA_Q_RANK, MLA_HEADS * (MLA_NOPE + MLA_ROPE)), MLA_Q_RANK ** -0.5),
        "mla_g_q_nope": gain((LA, MLA_NOPE)),
        "mla_g_q_pe": gain((LA, MLA_ROPE)),
        "mla_w_dkv": nrm((LA, D, MLA_KV_RANK + MLA_ROPE), D ** -0.5),
        "mla_g_dkv": gain((LA, MLA_KV_RANK)),
        "mla_g_k_pe": gain((LA, MLA_ROPE)),
        "mla_w_ukv": nrm((LA, MLA_KV_RANK, MLA_HEADS * (MLA_NOPE + MLA_V)), MLA_KV_RANK ** -0.5),
        "mla_g_k_nope": gain((LA, MLA_NOPE)),
        "mla_w_o": nrm((LA, MLA_HEADS * MLA_V, D), (MLA_HEADS * MLA_V) ** -0.5),
        "gqa_w_q": nrm((LB, D, GQA_HEADS * GQA_HEAD_DIM), D ** -0.5),
        "gqa_g_q": gain((LB, GQA_HEAD_DIM)),
        "gqa_w_kv": nrm((LB, D, 2 * GQA_KV_HEADS * GQA_HEAD_DIM), D ** -0.5),
        "gqa_g_k": gain((LB, GQA_HEAD_DIM)),
        "gqa_w_o": nrm((LB, GQA_HEADS * GQA_HEAD_DIM, D), (GQA_HEADS * GQA_HEAD_DIM) ** -0.5),
        "ffn_w_up": nrm((L, D, 2 * D_FF), D ** -0.5),
        "ffn_conv_w": nrm((L, CONV_W, D_FF), CONV_W ** -0.5),
        "ffn_conv_b": nrm((L, D_FF), 0.01),
        "ffn_w_down": nrm((L, D_FF, D), D_FF ** -0.5),
    }


def _fwd_reference(x, c, ctx, c_ctx, w_mod, b_mod, norm_mix, norm_ffn,
              mla_w_dq, mla_g_dq, mla_w_uq, mla_g_q_nope, mla_g_q_pe,
              mla_w_dkv, mla_g_dkv, mla_g_k_pe, mla_w_ukv, mla_g_k_nope, mla_w_o,
              gqa_w_q, gqa_g_q, gqa_w_kv, gqa_g_k, gqa_w_o,
              ffn_w_up, ffn_conv_w, ffn_conv_b, ffn_w_down):
    B, S, _ = x.shape
    C = ctx.shape[1]
    ROWS = S // GRID_W
    rows = jnp.repeat(jnp.arange(ROWS, dtype=jnp.int32), GRID_W)
    cols = jnp.tile(jnp.arange(GRID_W, dtype=jnp.int32), ROWS)
    rope_mla = axial_rope_tables(rows, cols, MLA_ROPE)
    rope_gqa = axial_rope_tables(rows, cols, GQA_HEAD_DIM)
    silu_c = jax.nn.silu(c)
    silu_cc = jax.nn.silu(c_ctx)

    for i in range(DEPTH):
        last = i == DEPTH - 1
        j = i // N_MIXERS
        mod = (silu_c @ w_mod[i] + b_mod[i])[:, None, :]
        mod_c = silu_cc @ w_mod[i] + b_mod[i]
        sh1, sc1, g1, sh2, sc2, g2 = jnp.split(mod, N_MOD, axis=-1)
        csh1, csc1, cg1, csh2, csc2, cg2 = jnp.split(mod_c, N_MOD, axis=-1)

        h = modulate(x, norm_mix[i], sh1, sc1)
        hc = modulate(ctx, norm_mix[i], csh1, csc1)
        if i % N_MIXERS == 0:
            q_fn = functools.partial(mla_queries, w_dq=mla_w_dq[j], g_dq=mla_g_dq[j], w_uq=mla_w_uq[j],
                                     g_q_nope=mla_g_q_nope[j], g_q_pe=mla_g_q_pe[j])
            kv_fn = functools.partial(mla_keys_values, w_dkv=mla_w_dkv[j], g_dkv=mla_g_dkv[j],
                                      g_k_pe=mla_g_k_pe[j], w_ukv=mla_w_ukv[j], g_k_nope=mla_g_k_nope[j])
            w_o, rope, scale = mla_w_o[j], rope_mla, MLA_SCALE
        else:
            q_fn = functools.partial(gqa_queries, w_q=gqa_w_q[j], g_q=gqa_g_q[j])
            kv_fn = functools.partial(gqa_keys_values, w_kv=gqa_w_kv[j], g_k=gqa_g_k[j])
            w_o, rope, scale = gqa_w_o[j], rope_gqa, GQA_SCALE

        k_lat, v_lat = kv_fn(h, rope)
        k_ctx, v_ctx = kv_fn(hc, None)
        o = blocked_attention(q_fn(h, rope),
                              jnp.concatenate([k_lat, k_ctx], axis=1),
                              jnp.concatenate([v_lat, v_ctx], axis=1), scale)
        x = x + g1 * (o.reshape(B, S, -1) @ w_o)
        if not last:
            oc = attend(q_fn(hc, None), k_ctx, v_ctx, scale)
            ctx = ctx + cg1 * (oc.reshape(B, C, -1) @ w_o)

        x = x + g2 * conv_ffn(modulate(x, norm_ffn[i], sh2, sc2),
                              ffn_w_up[i], ffn_conv_w[i], ffn_conv_b[i], ffn_w_down[i])
        if not last:
            ctx = ctx + cg2 * conv_ffn(modulate(ctx, norm_ffn[i], csh2, csc2),
                                       ffn_w_up[i], ffn_conv_w[i], ffn_conv_b[i], ffn_w_down[i])
    return x


import jax as _jax
import jax.numpy as _jnp

TWIN_FORMAT = 'train_step'
FWD_PARAMS = ['x', 'c', 'ctx', 'c_ctx', 'w_mod', 'b_mod', 'norm_mix', 'norm_ffn', 'mla_w_dq', 'mla_g_dq', 'mla_w_uq', 'mla_g_q_nope', 'mla_g_q_pe', 'mla_w_dkv', 'mla_g_dkv', 'mla_g_k_pe', 'mla_w_ukv', 'mla_g_k_nope', 'mla_w_o', 'gqa_w_q', 'gqa_g_q', 'gqa_w_kv', 'gqa_g_k', 'gqa_w_o', 'ffn_w_up', 'ffn_conv_w', 'ffn_conv_b', 'ffn_w_down']
TWIN_WEIGHTS = ['c_ctx', 'w_mod', 'b_mod', 'norm_mix', 'norm_ffn', 'mla_w_dq', 'mla_g_dq', 'mla_w_uq', 'mla_g_q_nope', 'mla_g_q_pe', 'mla_w_dkv', 'mla_g_dkv', 'mla_g_k_pe', 'mla_w_ukv', 'mla_g_k_nope', 'mla_w_o', 'gqa_w_q', 'gqa_g_q', 'gqa_w_kv', 'gqa_g_k', 'gqa_w_o', 'ffn_w_up', 'ffn_conv_w', 'ffn_conv_b', 'ffn_w_down']
TWIN_DIFF_INPUT = 'x'
TWIN_INPUTS = ['x', 'c', 'ctx', 'c_ctx', 'w_mod', 'b_mod', 'norm_mix', 'norm_ffn', 'mla_w_dq', 'mla_g_dq', 'mla_w_uq', 'mla_g_q_nope', 'mla_g_q_pe', 'mla_w_dkv', 'mla_g_dkv', 'mla_g_k_pe', 'mla_w_ukv', 'mla_g_k_nope', 'mla_w_o', 'gqa_w_q', 'gqa_g_q', 'gqa_w_kv', 'gqa_g_k', 'gqa_w_o', 'ffn_w_up', 'ffn_conv_w', 'ffn_conv_b', 'ffn_w_down', 'loss_target', 'm_c_ctx', 'm_w_mod', 'm_b_mod', 'm_norm_mix', 'm_norm_ffn', 'm_mla_w_dq', 'm_mla_g_dq', 'm_mla_w_uq', 'm_mla_g_q_nope', 'm_mla_g_q_pe', 'm_mla_w_dkv', 'm_mla_g_dkv', 'm_mla_g_k_pe', 'm_mla_w_ukv', 'm_mla_g_k_nope', 'm_mla_w_o', 'm_gqa_w_q', 'm_gqa_g_q', 'm_gqa_w_kv', 'm_gqa_g_k', 'm_gqa_w_o', 'm_ffn_w_up', 'm_ffn_conv_w', 'm_ffn_conv_b', 'm_ffn_w_down', 'v_c_ctx', 'v_w_mod', 'v_b_mod', 'v_norm_mix', 'v_norm_ffn', 'v_mla_w_dq', 'v_mla_g_dq', 'v_mla_w_uq', 'v_mla_g_q_nope', 'v_mla_g_q_pe', 'v_mla_w_dkv', 'v_mla_g_dkv', 'v_mla_g_k_pe', 'v_mla_w_ukv', 'v_mla_g_k_nope', 'v_mla_w_o', 'v_gqa_w_q', 'v_gqa_g_q', 'v_gqa_w_kv', 'v_gqa_g_k', 'v_gqa_w_o', 'v_ffn_w_up', 'v_ffn_conv_w', 'v_ffn_conv_b', 'v_ffn_w_down']
TWIN_OUTPUTS = ['loss', 'grad_x', 'grad_c_ctx', 'grad_w_mod', 'grad_b_mod', 'grad_norm_mix', 'grad_norm_ffn', 'grad_mla_w_dq', 'grad_mla_g_dq', 'grad_mla_w_uq', 'grad_mla_g_q_nope', 'grad_mla_g_q_pe', 'grad_mla_w_dkv', 'grad_mla_g_dkv', 'grad_mla_g_k_pe', 'grad_mla_w_ukv', 'grad_mla_g_k_nope', 'grad_mla_w_o', 'grad_gqa_w_q', 'grad_gqa_g_q', 'grad_gqa_w_kv', 'grad_gqa_g_k', 'grad_gqa_w_o', 'grad_ffn_w_up', 'grad_ffn_conv_w', 'grad_ffn_conv_b', 'grad_ffn_w_down', 'delta_c_ctx', 'delta_w_mod', 'delta_b_mod', 'delta_norm_mix', 'delta_norm_ffn', 'delta_mla_w_dq', 'delta_mla_g_dq', 'delta_mla_w_uq', 'delta_mla_g_q_nope', 'delta_mla_g_q_pe', 'delta_mla_w_dkv', 'delta_mla_g_dkv', 'delta_mla_g_k_pe', 'delta_mla_w_ukv', 'delta_mla_g_k_nope', 'delta_mla_w_o', 'delta_gqa_w_q', 'delta_gqa_g_q', 'delta_gqa_w_kv', 'delta_gqa_g_k', 'delta_gqa_w_o', 'delta_ffn_w_up', 'delta_ffn_conv_w', 'delta_ffn_conv_b', 'delta_ffn_w_down', 'new_m_c_ctx', 'new_m_w_mod', 'new_m_b_mod', 'new_m_norm_mix', 'new_m_norm_ffn', 'new_m_mla_w_dq', 'new_m_mla_g_dq', 'new_m_mla_w_uq', 'new_m_mla_g_q_nope', 'new_m_mla_g_q_pe', 'new_m_mla_w_dkv', 'new_m_mla_g_dkv', 'new_m_mla_g_k_pe', 'new_m_mla_w_ukv', 'new_m_mla_g_k_nope', 'new_m_mla_w_o', 'new_m_gqa_w_q', 'new_m_gqa_g_q', 'new_m_gqa_w_kv', 'new_m_gqa_g_k', 'new_m_gqa_w_o', 'new_m_ffn_w_up', 'new_m_ffn_conv_w', 'new_m_ffn_conv_b', 'new_m_ffn_w_down', 'new_v_c_ctx', 'new_v_w_mod', 'new_v_b_mod', 'new_v_norm_mix', 'new_v_norm_ffn', 'new_v_mla_w_dq', 'new_v_mla_g_dq', 'new_v_mla_w_uq', 'new_v_mla_g_q_nope', 'new_v_mla_g_q_pe', 'new_v_mla_w_dkv', 'new_v_mla_g_dkv', 'new_v_mla_g_k_pe', 'new_v_mla_w_ukv', 'new_v_mla_g_k_nope', 'new_v_mla_w_o', 'new_v_gqa_w_q', 'new_v_gqa_g_q', 'new_v_gqa_w_kv', 'new_v_gqa_g_k', 'new_v_gqa_w_o', 'new_v_ffn_w_up', 'new_v_ffn_conv_w', 'new_v_ffn_conv_b', 'new_v_ffn_w_down']
TWIN_LEAF_KINDS = {'loss': 'loss', 'grad_x': 'grad_x', 'grad_c_ctx': 'grad_w', 'grad_w_mod': 'grad_w', 'grad_b_mod': 'grad_w', 'grad_norm_mix': 'grad_w', 'grad_norm_ffn': 'grad_w', 'grad_mla_w_dq': 'grad_w', 'grad_mla_g_dq': 'grad_w', 'grad_mla_w_uq': 'grad_w', 'grad_mla_g_q_nope': 'grad_w', 'grad_mla_g_q_pe': 'grad_w', 'grad_mla_w_dkv': 'grad_w', 'grad_mla_g_dkv': 'grad_w', 'grad_mla_g_k_pe': 'grad_w', 'grad_mla_w_ukv': 'grad_w', 'grad_mla_g_k_nope': 'grad_w', 'grad_mla_w_o': 'grad_w', 'grad_gqa_w_q': 'grad_w', 'grad_gqa_g_q': 'grad_w', 'grad_gqa_w_kv': 'grad_w', 'grad_gqa_g_k': 'grad_w', 'grad_gqa_w_o': 'grad_w', 'grad_ffn_w_up': 'grad_w', 'grad_ffn_conv_w': 'grad_w', 'grad_ffn_conv_b': 'grad_w', 'grad_ffn_w_down': 'grad_w', 'delta_c_ctx': 'delta_w', 'delta_w_mod': 'delta_w', 'delta_b_mod': 'delta_w', 'delta_norm_mix': 'delta_w', 'delta_norm_ffn': 'delta_w', 'delta_mla_w_dq': 'delta_w', 'delta_mla_g_dq': 'delta_w', 'delta_mla_w_uq': 'delta_w', 'delta_mla_g_q_nope': 'delta_w', 'delta_mla_g_q_pe': 'delta_w', 'delta_mla_w_dkv': 'delta_w', 'delta_mla_g_dkv': 'delta_w', 'delta_mla_g_k_pe': 'delta_w', 'delta_mla_w_ukv': 'delta_w', 'delta_mla_g_k_nope': 'delta_w', 'delta_mla_w_o': 'delta_w', 'delta_gqa_w_q': 'delta_w', 'delta_gqa_g_q': 'delta_w', 'delta_gqa_w_kv': 'delta_w', 'delta_gqa_g_k': 'delta_w', 'delta_gqa_w_o': 'delta_w', 'delta_ffn_w_up': 'delta_w', 'delta_ffn_conv_w': 'delta_w', 'delta_ffn_conv_b': 'delta_w', 'delta_ffn_w_down': 'delta_w', 'new_m_c_ctx': 'new_m', 'new_m_w_mod': 'new_m', 'new_m_b_mod': 'new_m', 'new_m_norm_mix': 'new_m', 'new_m_norm_ffn': 'new_m', 'new_m_mla_w_dq': 'new_m', 'new_m_mla_g_dq': 'new_m', 'new_m_mla_w_uq': 'new_m', 'new_m_mla_g_q_nope': 'new_m', 'new_m_mla_g_q_pe': 'new_m', 'new_m_mla_w_dkv': 'new_m', 'new_m_mla_g_dkv': 'new_m', 'new_m_mla_g_k_pe': 'new_m', 'new_m_mla_w_ukv': 'new_m', 'new_m_mla_g_k_nope': 'new_m', 'new_m_mla_w_o': 'new_m', 'new_m_gqa_w_q': 'new_m', 'new_m_gqa_g_q': 'new_m', 'new_m_gqa_w_kv': 'new_m', 'new_m_gqa_g_k': 'new_m', 'new_m_gqa_w_o': 'new_m', 'new_m_ffn_w_up': 'new_m', 'new_m_ffn_conv_w': 'new_m', 'new_m_ffn_conv_b': 'new_m', 'new_m_ffn_w_down': 'new_m', 'new_v_c_ctx': 'new_v', 'new_v_w_mod': 'new_v', 'new_v_b_mod': 'new_v', 'new_v_norm_mix': 'new_v', 'new_v_norm_ffn': 'new_v', 'new_v_mla_w_dq': 'new_v', 'new_v_mla_g_dq': 'new_v', 'new_v_mla_w_uq': 'new_v', 'new_v_mla_g_q_nope': 'new_v', 'new_v_mla_g_q_pe': 'new_v', 'new_v_mla_w_dkv': 'new_v', 'new_v_mla_g_dkv': 'new_v', 'new_v_mla_g_k_pe': 'new_v', 'new_v_mla_w_ukv': 'new_v', 'new_v_mla_g_k_nope': 'new_v', 'new_v_mla_w_o': 'new_v', 'new_v_gqa_w_q': 'new_v', 'new_v_gqa_g_q': 'new_v', 'new_v_gqa_w_kv': 'new_v', 'new_v_gqa_g_k': 'new_v', 'new_v_gqa_w_o': 'new_v', 'new_v_ffn_w_up': 'new_v', 'new_v_ffn_conv_w': 'new_v', 'new_v_ffn_conv_b': 'new_v', 'new_v_ffn_w_down': 'new_v'}


def _forward(args):
    return _fwd_reference(*[args[k] for k in FWD_PARAMS])


def _output_shape():
    def fwd():
        inp = _fwd_setup_inputs(0)
        return _fwd_reference(*[inp[k] for k in FWD_PARAMS])
    out = _jax.eval_shape(fwd)
    return out.shape, out.dtype

N_MICROBATCH = 1
ADAM_LR = 0.001
ADAM_B1 = 0.9
ADAM_B2 = 0.999
ADAM_EPS = 1e-08
ADAM_WD = 0.01
ADAM_STEP = 10
PER_EXAMPLE_BATCH_AXIS = {'x': 0, 'c': 0, 'ctx': 0, 'loss_target': 0}
SHARED_INPUTS = []
_WEIGHT_DTYPES = {'c_ctx': _jnp.float32, 'w_mod': _jnp.float32, 'b_mod': _jnp.float32, 'norm_mix': _jnp.float32, 'norm_ffn': _jnp.float32, 'mla_w_dq': _jnp.float32, 'mla_g_dq': _jnp.float32, 'mla_w_uq': _jnp.float32, 'mla_g_q_nope': _jnp.float32, 'mla_g_q_pe': _jnp.float32, 'mla_w_dkv': _jnp.float32, 'mla_g_dkv': _jnp.float32, 'mla_g_k_pe': _jnp.float32, 'mla_w_ukv': _jnp.float32, 'mla_g_k_nope': _jnp.float32, 'mla_w_o': _jnp.float32, 'gqa_w_q': _jnp.float32, 'gqa_g_q': _jnp.float32, 'gqa_w_kv': _jnp.float32, 'gqa_g_k': _jnp.float32, 'gqa_w_o': _jnp.float32, 'ffn_w_up': _jnp.float32, 'ffn_conv_w': _jnp.float32, 'ffn_conv_b': _jnp.float32, 'ffn_w_down': _jnp.float32}
MOMENT_SCALE = {'c_ctx': 1.414923e-01, 'w_mod': 6.287053e-01, 'b_mod': 1.633349e+00, 'norm_mix': 1.474010e-01, 'norm_ffn': 3.141238e+00, 'mla_w_dq': 2.142117e-02, 'mla_g_dq': 2.118799e-02, 'mla_w_uq': 8.374064e-03, 'mla_g_q_nope': 4.770847e-02, 'mla_g_q_pe': 3.506617e-02, 'mla_w_dkv': 4.338606e-01, 'mla_g_dkv': 1.019959e+00, 'mla_g_k_pe': 3.542444e-02, 'mla_w_ukv': 1.509456e-01, 'mla_g_k_nope': 4.789769e-02, 'mla_w_o': 2.174813e-01, 'gqa_w_q': 7.306669e-03, 'gqa_g_q': 5.584800e-02, 'gqa_w_kv': 3.431068e-01, 'gqa_g_k': 5.616336e-02, 'gqa_w_o': 2.229810e-01, 'ffn_w_up': 6.156355e-02, 'ffn_conv_w': 3.727113e-01, 'ffn_conv_b': 4.184998e-01, 'ffn_w_down': 6.792114e-02}


def _to_microbatches(a, axis):
    t = _jnp.moveaxis(a, axis, 0)
    t = t.reshape((N_MICROBATCH, t.shape[0] // N_MICROBATCH) + t.shape[1:])
    return _jnp.moveaxis(t, 1, axis + 1)


def setup_inputs(seed: int = 0) -> dict:
    inp = _fwd_setup_inputs(seed)
    key = _jax.random.fold_in(_jax.random.key(seed), 7919)
    shape, _ = _output_shape()
    out = dict(inp)
    out["loss_target"] = _jax.random.normal(_jax.random.fold_in(key, 0), shape, _jnp.float32)
    for i, name in enumerate(TWIN_WEIGHTS):
        w = inp[name].astype(_jnp.float32)
        if MOMENT_SCALE is None:
            s = _jnp.sqrt(_jnp.mean(_jnp.square(w)) + 1e-30)
        else:
            s = MOMENT_SCALE[name]
        km, kv = _jax.random.split(_jax.random.fold_in(key, i + 1))
        out[name] = w
        out["m_" + name] = s * _jax.random.normal(km, w.shape, _jnp.float32)
        out["v_" + name] = (s * s) * _jax.random.uniform(kv, w.shape, _jnp.float32, 0.5, 1.5)
    if N_MICROBATCH > 1:
        for name, axis in PER_EXAMPLE_BATCH_AXIS.items():
            out[name] = _to_microbatches(out[name], axis)
    return {'x': out['x'], 'c': out['c'], 'ctx': out['ctx'], 'c_ctx': out['c_ctx'], 'w_mod': out['w_mod'], 'b_mod': out['b_mod'], 'norm_mix': out['norm_mix'], 'norm_ffn': out['norm_ffn'], 'mla_w_dq': out['mla_w_dq'], 'mla_g_dq': out['mla_g_dq'], 'mla_w_uq': out['mla_w_uq'], 'mla_g_q_nope': out['mla_g_q_nope'], 'mla_g_q_pe': out['mla_g_q_pe'], 'mla_w_dkv': out['mla_w_dkv'], 'mla_g_dkv': out['mla_g_dkv'], 'mla_g_k_pe': out['mla_g_k_pe'], 'mla_w_ukv': out['mla_w_ukv'], 'mla_g_k_nope': out['mla_g_k_nope'], 'mla_w_o': out['mla_w_o'], 'gqa_w_q': out['gqa_w_q'], 'gqa_g_q': out['gqa_g_q'], 'gqa_w_kv': out['gqa_w_kv'], 'gqa_g_k': out['gqa_g_k'], 'gqa_w_o': out['gqa_w_o'], 'ffn_w_up': out['ffn_w_up'], 'ffn_conv_w': out['ffn_conv_w'], 'ffn_conv_b': out['ffn_conv_b'], 'ffn_w_down': out['ffn_w_down'], 'loss_target': out['loss_target'], 'm_c_ctx': out['m_c_ctx'], 'm_w_mod': out['m_w_mod'], 'm_b_mod': out['m_b_mod'], 'm_norm_mix': out['m_norm_mix'], 'm_norm_ffn': out['m_norm_ffn'], 'm_mla_w_dq': out['m_mla_w_dq'], 'm_mla_g_dq': out['m_mla_g_dq'], 'm_mla_w_uq': out['m_mla_w_uq'], 'm_mla_g_q_nope': out['m_mla_g_q_nope'], 'm_mla_g_q_pe': out['m_mla_g_q_pe'], 'm_mla_w_dkv': out['m_mla_w_dkv'], 'm_mla_g_dkv': out['m_mla_g_dkv'], 'm_mla_g_k_pe': out['m_mla_g_k_pe'], 'm_mla_w_ukv': out['m_mla_w_ukv'], 'm_mla_g_k_nope': out['m_mla_g_k_nope'], 'm_mla_w_o': out['m_mla_w_o'], 'm_gqa_w_q': out['m_gqa_w_q'], 'm_gqa_g_q': out['m_gqa_g_q'], 'm_gqa_w_kv': out['m_gqa_w_kv'], 'm_gqa_g_k': out['m_gqa_g_k'], 'm_gqa_w_o': out['m_gqa_w_o'], 'm_ffn_w_up': out['m_ffn_w_up'], 'm_ffn_conv_w': out['m_ffn_conv_w'], 'm_ffn_conv_b': out['m_ffn_conv_b'], 'm_ffn_w_down': out['m_ffn_w_down'], 'v_c_ctx': out['v_c_ctx'], 'v_w_mod': out['v_w_mod'], 'v_b_mod': out['v_b_mod'], 'v_norm_mix': out['v_norm_mix'], 'v_norm_ffn': out['v_norm_ffn'], 'v_mla_w_dq': out['v_mla_w_dq'], 'v_mla_g_dq': out['v_mla_g_dq'], 'v_mla_w_uq': out['v_mla_w_uq'], 'v_mla_g_q_nope': out['v_mla_g_q_nope'], 'v_mla_g_q_pe': out['v_mla_g_q_pe'], 'v_mla_w_dkv': out['v_mla_w_dkv'], 'v_mla_g_dkv': out['v_mla_g_dkv'], 'v_mla_g_k_pe': out['v_mla_g_k_pe'], 'v_mla_w_ukv': out['v_mla_w_ukv'], 'v_mla_g_k_nope': out['v_mla_g_k_nope'], 'v_mla_w_o': out['v_mla_w_o'], 'v_gqa_w_q': out['v_gqa_w_q'], 'v_gqa_g_q': out['v_gqa_g_q'], 'v_gqa_w_kv': out['v_gqa_w_kv'], 'v_gqa_g_k': out['v_gqa_g_k'], 'v_gqa_w_o': out['v_gqa_w_o'], 'v_ffn_w_up': out['v_ffn_w_up'], 'v_ffn_conv_w': out['v_ffn_conv_w'], 'v_ffn_conv_b': out['v_ffn_conv_b'], 'v_ffn_w_down': out['v_ffn_w_down']}


def _loss(weights, diff, rest, loss_target):
    with _jax.named_scope("forward"):
        args = {**rest, TWIN_DIFF_INPUT: diff, **{k: w.astype(_WEIGHT_DTYPES[k]) for k, w in weights.items()}}
        y = _forward(args)
    with _jax.named_scope("loss_head"):
        err = _jnp.square(y.astype(_jnp.float32) - loss_target)
        return 0.5 * _jnp.sum(_jnp.mean(err, axis=-1)) if err.ndim else 0.5 * err


def _adamw(w, g, m, v):
    m = ADAM_B1 * m + (1.0 - ADAM_B1) * g
    v = ADAM_B2 * v + (1.0 - ADAM_B2) * _jnp.square(g)
    m_hat = m / (1.0 - ADAM_B1 ** ADAM_STEP)
    v_hat = v / (1.0 - ADAM_B2 ** ADAM_STEP)
    delta = -ADAM_LR * (m_hat / (_jnp.sqrt(v_hat) + ADAM_EPS) + ADAM_WD * w)
    return delta, m, v


def reference(x, c, ctx, c_ctx, w_mod, b_mod, norm_mix, norm_ffn, mla_w_dq, mla_g_dq, mla_w_uq, mla_g_q_nope, mla_g_q_pe, mla_w_dkv, mla_g_dkv, mla_g_k_pe, mla_w_ukv, mla_g_k_nope, mla_w_o, gqa_w_q, gqa_g_q, gqa_w_kv, gqa_g_k, gqa_w_o, ffn_w_up, ffn_conv_w, ffn_conv_b, ffn_w_down, loss_target, m_c_ctx, m_w_mod, m_b_mod, m_norm_mix, m_norm_ffn, m_mla_w_dq, m_mla_g_dq, m_mla_w_uq, m_mla_g_q_nope, m_mla_g_q_pe, m_mla_w_dkv, m_mla_g_dkv, m_mla_g_k_pe, m_mla_w_ukv, m_mla_g_k_nope, m_mla_w_o, m_gqa_w_q, m_gqa_g_q, m_gqa_w_kv, m_gqa_g_k, m_gqa_w_o, m_ffn_w_up, m_ffn_conv_w, m_ffn_conv_b, m_ffn_w_down, v_c_ctx, v_w_mod, v_b_mod, v_norm_mix, v_norm_ffn, v_mla_w_dq, v_mla_g_dq, v_mla_w_uq, v_mla_g_q_nope, v_mla_g_q_pe, v_mla_w_dkv, v_mla_g_dkv, v_mla_g_k_pe, v_mla_w_ukv, v_mla_g_k_nope, v_mla_w_o, v_gqa_w_q, v_gqa_g_q, v_gqa_w_kv, v_gqa_g_k, v_gqa_w_o, v_ffn_w_up, v_ffn_conv_w, v_ffn_conv_b, v_ffn_w_down):
    given = dict(x=x, c=c, ctx=ctx, c_ctx=c_ctx, w_mod=w_mod, b_mod=b_mod, norm_mix=norm_mix, norm_ffn=norm_ffn, mla_w_dq=mla_w_dq, mla_g_dq=mla_g_dq, mla_w_uq=mla_w_uq, mla_g_q_nope=mla_g_q_nope, mla_g_q_pe=mla_g_q_pe, mla_w_dkv=mla_w_dkv, mla_g_dkv=mla_g_dkv, mla_g_k_pe=mla_g_k_pe, mla_w_ukv=mla_w_ukv, mla_g_k_nope=mla_g_k_nope, mla_w_o=mla_w_o, gqa_w_q=gqa_w_q, gqa_g_q=gqa_g_q, gqa_w_kv=gqa_w_kv, gqa_g_k=gqa_g_k, gqa_w_o=gqa_w_o, ffn_w_up=ffn_w_up, ffn_conv_w=ffn_conv_w, ffn_conv_b=ffn_conv_b, ffn_w_down=ffn_w_down, loss_target=loss_target, m_c_ctx=m_c_ctx, m_w_mod=m_w_mod, m_b_mod=m_b_mod, m_norm_mix=m_norm_mix, m_norm_ffn=m_norm_ffn, m_mla_w_dq=m_mla_w_dq, m_mla_g_dq=m_mla_g_dq, m_mla_w_uq=m_mla_w_uq, m_mla_g_q_nope=m_mla_g_q_nope, m_mla_g_q_pe=m_mla_g_q_pe, m_mla_w_dkv=m_mla_w_dkv, m_mla_g_dkv=m_mla_g_dkv, m_mla_g_k_pe=m_mla_g_k_pe, m_mla_w_ukv=m_mla_w_ukv, m_mla_g_k_nope=m_mla_g_k_nope, m_mla_w_o=m_mla_w_o, m_gqa_w_q=m_gqa_w_q, m_gqa_g_q=m_gqa_g_q, m_gqa_w_kv=m_gqa_w_kv, m_gqa_g_k=m_gqa_g_k, m_gqa_w_o=m_gqa_w_o, m_ffn_w_up=m_ffn_w_up, m_ffn_conv_w=m_ffn_conv_w, m_ffn_conv_b=m_ffn_conv_b, m_ffn_w_down=m_ffn_w_down, v_c_ctx=v_c_ctx, v_w_mod=v_w_mod, v_b_mod=v_b_mod, v_norm_mix=v_norm_mix, v_norm_ffn=v_norm_ffn, v_mla_w_dq=v_mla_w_dq, v_mla_g_dq=v_mla_g_dq, v_mla_w_uq=v_mla_w_uq, v_mla_g_q_nope=v_mla_g_q_nope, v_mla_g_q_pe=v_mla_g_q_pe, v_mla_w_dkv=v_mla_w_dkv, v_mla_g_dkv=v_mla_g_dkv, v_mla_g_k_pe=v_mla_g_k_pe, v_mla_w_ukv=v_mla_w_ukv, v_mla_g_k_nope=v_mla_g_k_nope, v_mla_w_o=v_mla_w_o, v_gqa_w_q=v_gqa_w_q, v_gqa_g_q=v_gqa_g_q, v_gqa_w_kv=v_gqa_w_kv, v_gqa_g_k=v_gqa_g_k, v_gqa_w_o=v_gqa_w_o, v_ffn_w_up=v_ffn_w_up, v_ffn_conv_w=v_ffn_conv_w, v_ffn_conv_b=v_ffn_conv_b, v_ffn_w_down=v_ffn_w_down)
    weights = {n: given[n] for n in TWIN_WEIGHTS}
    shared = {n: given[n] for n in SHARED_INPUTS}
    per_example = {n: given[n] for n in ['x', 'c', 'ctx']}
    grad_fn = _jax.value_and_grad(_loss, argnums=(0, 1))

    def one_microbatch(ex, loss_target):
        ex = dict(ex)
        diff = ex.pop(TWIN_DIFF_INPUT)
        return grad_fn(weights, diff, {**shared, **ex}, loss_target)

    if N_MICROBATCH == 1:
        loss, (grad_w, grad_x) = one_microbatch(per_example, given["loss_target"])
    else:
        def body(carry, xs):
            loss_sum, grad_sum = carry
            l_k, (gw_k, gx_k) = one_microbatch(xs[0], xs[1])
            with _jax.named_scope("update"):
                return (loss_sum + l_k, _jax.tree.map(_jnp.add, grad_sum, gw_k)), gx_k

        init = (_jnp.zeros((), _jnp.float32), _jax.tree.map(_jnp.zeros_like, weights))
        (loss, grad_w), grad_x = _jax.lax.scan(body, init, (per_example, given["loss_target"]))
    with _jax.named_scope("update"):
        delta_w, new_m, new_v = {}, {}, {}
        for n in TWIN_WEIGHTS:
            delta_w[n], new_m[n], new_v[n] = _adamw(weights[n], grad_w[n], given["m_" + n], given["v_" + n])
    return (loss, grad_x, *[grad_w[n] for n in TWIN_WEIGHTS], *[delta_w[n] for n in TWIN_WEIGHTS],
            *[new_m[n] for n in TWIN_WEIGHTS], *[new_v[n] for n in TWIN_WEIGHTS])
```

```python
import functools
import math

import jax
import jax.numpy as jnp
from jax import lax
from jax.experimental import pallas as pl
from jax.experimental.pallas import tpu as pltpu

F32, BF16 = jnp.float32, jnp.bfloat16
MESH = pl.DeviceIdType.MESH

EPS = 1e-6
ROPE_BASE = 10000.0
GRID_W = 64
NOPE, ROPE_D, VDIM = 128, 64, 128
HEAD = 128
N_MOD = 6
LANES = 128
HALO = 16
VMEM_LIMIT = 56 * 1024 * 1024
PACK_W = 1024

ADAM_LR, ADAM_B1, ADAM_B2, ADAM_EPS, ADAM_WD, ADAM_STEP = 0.001, 0.9, 0.999, 1e-08, 0.01, 10

LANE_TILES = (1024, 768, 512, 256, 128)
ROW_TILES = (256, 128, 64, 32, 16, 8)


def _pick(n, prefs):
    for p in prefs:
        if n % p == 0:
            return p
    return n


def _params(*sem):
    return pltpu.CompilerParams(dimension_semantics=sem, vmem_limit_bytes=VMEM_LIMIT)


def _mm(a, b, *, ta=False, tb=False, out_dtype=F32, resid=None, gate=None, mask=None, name="mm"):
    (K, M) = a.shape if ta else a.shape[::-1]
    N = b.shape[0] if tb else b.shape[1]
    assert (b.shape[1] if tb else b.shape[0]) == K
    tm, tn, tk = _pick(M, LANE_TILES), _pick(N, LANE_TILES), _pick(K, LANE_TILES)
    nk = K // tk
    fused = resid is not None
    dn = (((0 if ta else 1,), (1 if tb else 0,)), ((), ()))

    def body(*refs):
        if fused:
            a_ref, b_ref, r_ref, g_ref, m_ref, o_ref, y_ref, acc_ref = refs
        else:
            a_ref, b_ref, o_ref, acc_ref = refs
        k = pl.program_id(2)

        @pl.when(k == 0)
        def _():
            acc_ref[...] = jnp.zeros_like(acc_ref)

        acc_ref[...] += lax.dot_general(a_ref[...].astype(BF16), b_ref[...].astype(BF16), dn,
                                        preferred_element_type=F32)

        @pl.when(k == nk - 1)
        def _():
            acc = acc_ref[...]
            if fused:
                g = jnp.where(m_ref[...] > 0.5, g_ref[1:2, :], g_ref[0:1, :])
                y_ref[...] = acc.astype(y_ref.dtype)
                o_ref[...] = r_ref[...] + g * acc
            else:
                o_ref[...] = acc.astype(o_ref.dtype)

    a_spec = pl.BlockSpec((tk, tm), lambda i, j, k: (k, i)) if ta else pl.BlockSpec((tm, tk), lambda i, j, k: (i, k))
    b_spec = pl.BlockSpec((tn, tk), lambda i, j, k: (j, k)) if tb else pl.BlockSpec((tk, tn), lambda i, j, k: (k, j))
    o_spec = pl.BlockSpec((tm, tn), lambda i, j, k: (i, j))
    in_specs, args = [a_spec, b_spec], [a, b]
    out_shape, out_specs = jax.ShapeDtypeStruct((M, N), out_dtype), o_spec
    if fused:
        in_specs += [o_spec, pl.BlockSpec((2, tn), lambda i, j, k: (0, j)), pl.BlockSpec((tm, 1), lambda i, j, k: (i, 0))]
        args += [resid, gate, mask]
        out_shape = (jax.ShapeDtypeStruct((M, N), F32), jax.ShapeDtypeStruct((M, N), BF16))
        out_specs = (o_spec, o_spec)
    return pl.pallas_call(
        body, name=name, grid=(M // tm, N // tn, nk), in_specs=in_specs, out_specs=out_specs, out_shape=out_shape,
        scratch_shapes=[pltpu.VMEM((tm, tn), F32)],
        compiler_params=_params("parallel", "parallel", "arbitrary"),
    )(*args)


@jax.custom_vjp
def linear(a, w):
    return _mm(a, w, out_dtype=F32, name="lin_fwd")


def _linear_fwd(a, w):
    return linear(a, w), (a, w)


def _linear_bwd(res, dy):
    a, w = res
    return (_mm(dy, w, tb=True, out_dtype=a.dtype, name="lin_da"),
            _mm(a, dy, ta=True, out_dtype=w.dtype, name="lin_dw"))


linear.defvjp(_linear_fwd, _linear_bwd)


@jax.custom_vjp
def linear_bf16(a, w):
    return _mm(a, w, out_dtype=BF16, name="linb_fwd")


def _linear_bf16_fwd(a, w):
    return linear_bf16(a, w), (a, w)


linear_bf16.defvjp(_linear_bf16_fwd, _linear_bwd)


def _gate_bwd(dx, y, gate, mask):
    T, N = dx.shape
    tb = _pick(T, ROW_TILES)

    def body(dx_ref, y_ref, g_ref, m_ref, dy_ref, dg_ref):
        @pl.when(pl.program_id(0) == 0)
        def _():
            dg_ref[...] = jnp.zeros_like(dg_ref)

        d = dx_ref[...]
        ctx = m_ref[...] > 0.5
        dy_ref[...] = (jnp.where(ctx, g_ref[1:2, :], g_ref[0:1, :]) * d).astype(dy_ref.dtype)
        dyy = d * y_ref[...].astype(F32)
        dg_ref[0:1, :] += jnp.sum(jnp.where(ctx, 0.0, dyy), axis=0, keepdims=True)
        dg_ref[1:2, :] += jnp.sum(jnp.where(ctx, dyy, 0.0), axis=0, keepdims=True)

    row = pl.BlockSpec((tb, N), lambda i: (i, 0))
    small = pl.BlockSpec((2, N), lambda i: (0, 0))
    return pl.pallas_call(
        body, name="gate_bwd", grid=(T // tb,),
        in_specs=[row, row, small, pl.BlockSpec((tb, 1), lambda i: (i, 0))], out_specs=(row, small),
        out_shape=(jax.ShapeDtypeStruct((T, N), BF16), jax.ShapeDtypeStruct((2, N), F32)),
        compiler_params=_params("arbitrary"),
    )(dx, y, gate, mask)


@jax.custom_vjp
def linear_gated(a, w, x, gate, mask):
    return _mm(a, w, resid=x, gate=gate, mask=mask, name="ling_fwd")[0]


def _linear_gated_fwd(a, w, x, gate, mask):
    out, y = _mm(a, w, resid=x, gate=gate, mask=mask, name="ling_fwd")
    return out, (a, w, y, gate, mask)


def _linear_gated_bwd(res, dx):
    a, w, y, gate, mask = res
    dy, dgate = _gate_bwd(dx, y, gate, mask)
    return (_mm(dy, w, tb=True, out_dtype=a.dtype, name="ling_da"),
            _mm(a, dy, ta=True, out_dtype=w.dtype, name="ling_dw"), dx, dgate, None)


linear_gated.defvjp(_linear_gated_fwd, _linear_gated_bwd)


def _rope_rot(x, quarter):
    lane = lax.broadcasted_iota(jnp.int32, x.shape, 1)
    lo = (lane % (2 * quarter)) < quarter
    return jnp.where(lo, -pltpu.roll(x, LANES - quarter, 1), pltpu.roll(x, quarter, 1))


def _group_mean(v, group):
    if group == LANES:
        return jnp.mean(v, axis=-1, keepdims=True)
    lane = lax.broadcasted_iota(jnp.int32, v.shape, 1)
    lo = lane < group
    s_lo = jnp.sum(jnp.where(lo, v, 0.0), axis=-1, keepdims=True)
    s_hi = jnp.sum(jnp.where(lo, 0.0, v), axis=-1, keepdims=True)
    return jnp.where(lo, s_lo, s_hi) * (1.0 / group)


def _norm_call(x, gain, dy, *, group, quarter, cos, sin, shift, scale, mask, out_dtype, name):
    T, W = x.shape
    whole = group == W
    use_rope, use_mod, bwd = cos is not None, shift is not None, dy is not None
    assert not (whole and use_rope) and not (use_mod and not whole)
    tb = _pick(T, ROW_TILES)
    nchunk = W // LANES

    def body(*refs):
        refs = list(refs)
        x_ref, g_ref = refs.pop(0), refs.pop(0)
        dy_ref = refs.pop(0) if bwd else None
        cos_ref, sin_ref = (refs.pop(0), refs.pop(0)) if use_rope else (None, None)
        sh_ref, sc_ref, m_ref = (refs.pop(0), refs.pop(0), refs.pop(0)) if use_mod else (None, None, None)
        if not bwd:
            (o_ref,) = refs
        elif use_mod:
            dx_ref, dg_ref, dsh_ref, dsc_ref = refs
        else:
            dx_ref, dg_ref = refs

        if bwd:
            @pl.when(pl.program_id(0) == 0)
            def _():
                dg_ref[...] = jnp.zeros_like(dg_ref)
                if use_mod:
                    dsh_ref[...] = jnp.zeros_like(dsh_ref)
                    dsc_ref[...] = jnp.zeros_like(dsc_ref)

        if whole:
            xv = x_ref[...].astype(F32)
            r = lax.rsqrt(jnp.mean(xv * xv, axis=-1, keepdims=True) + EPS)
            xn = xv * r
            gain_v = g_ref[...]
            if use_mod:
                ctx = m_ref[...] > 0.5
                sc = jnp.where(ctx, sc_ref[1:2, :], sc_ref[0:1, :])
            if not bwd:
                y = xn * gain_v
                if use_mod:
                    y = y * (1.0 + sc) + jnp.where(ctx, sh_ref[1:2, :], sh_ref[0:1, :])
                o_ref[...] = y.astype(o_ref.dtype)
            else:
                d = dy_ref[...].astype(F32)
                if use_mod:
                    dyy = d * (xn * gain_v)
                    dsh_ref[0:1, :] += jnp.sum(jnp.where(ctx, 0.0, d), axis=0, keepdims=True)
                    dsh_ref[1:2, :] += jnp.sum(jnp.where(ctx, d, 0.0), axis=0, keepdims=True)
                    dsc_ref[0:1, :] += jnp.sum(jnp.where(ctx, 0.0, dyy), axis=0, keepdims=True)
                    dsc_ref[1:2, :] += jnp.sum(jnp.where(ctx, dyy, 0.0), axis=0, keepdims=True)
                    d = d * (1.0 + sc)
                dg_ref[...] += jnp.sum(d * xn, axis=0, keepdims=True)
                dxn = d * gain_v
                dx_ref[...] = (r * (dxn - xn * jnp.mean(dxn * xn, axis=-1, keepdims=True))).astype(dx_ref.dtype)
        else:
            gain_v = g_ref[...]
            if use_rope:
                cs, sn = cos_ref[...], sin_ref[...]
            for c in range(nchunk):
                cols = slice(c * LANES, (c + 1) * LANES)
                xv = x_ref[:, cols].astype(F32)
                r = lax.rsqrt(_group_mean(xv * xv, group) + EPS)
                xn = xv * r
                if not bwd:
                    y = xn * gain_v
                    if use_rope:
                        y = y * cs + _rope_rot(y, quarter) * sn
                    o_ref[:, cols] = y.astype(o_ref.dtype)
                else:
                    d = dy_ref[:, cols].astype(F32)
                    if use_rope:
                        d = d * cs - _rope_rot(d * sn, quarter)
                    dg_ref[...] += jnp.sum(d * xn, axis=0, keepdims=True)
                    dxn = d * gain_v
                    dx_ref[:, cols] = (r * (dxn - xn * _group_mean(dxn * xn, group))).astype(dx_ref.dtype)

    row = pl.BlockSpec((tb, W), lambda i: (i, 0))
    gw = W if whole else LANES
    gspec = pl.BlockSpec((1, gw), lambda i: (0, 0))
    in_specs, args = [row, gspec], [x, gain]
    if bwd:
        in_specs.append(row)
        args.append(dy)
    if use_rope:
        tab = pl.BlockSpec((tb, LANES), lambda i: (i, 0))
        in_specs += [tab, tab]
        args += [cos, sin]
    if use_mod:
        two = pl.BlockSpec((2, W), lambda i: (0, 0))
        in_specs += [two, two, pl.BlockSpec((tb, 1), lambda i: (i, 0))]
        args += [shift, scale, mask]
    if not bwd:
        out_shape, out_specs = jax.ShapeDtypeStruct((T, W), out_dtype), row
    else:
        out_shape = [jax.ShapeDtypeStruct((T, W), x.dtype), jax.ShapeDtypeStruct((1, gw), F32)]
        out_specs = [row, gspec]
        if use_mod:
            out_shape += [jax.ShapeDtypeStruct((2, W), F32)] * 2
            out_specs += [pl.BlockSpec((2, W), lambda i: (0, 0))] * 2
    return pl.pallas_call(
        body, name=name, grid=(T // tb,), in_specs=in_specs, out_specs=out_specs, out_shape=out_shape,
        compiler_params=_params("arbitrary" if bwd else "parallel"),
    )(*args)


def make_norm(group, quarter=0, name="norm"):
    kw = dict(group=group, quarter=quarter, shift=None, scale=None, mask=None, out_dtype=BF16)

    @jax.custom_vjp
    def op(x, gain, cos, sin):
        return _norm_call(x, gain, None, cos=cos, sin=sin, name=name + "_fwd", **kw)

    def fwd(x, gain, cos, sin):
        return op(x, gain, cos, sin), (x, gain, cos, sin)

    def bwd(res, dy):
        x, gain, cos, sin = res
        dx, dg = _norm_call(x, gain, dy, cos=cos, sin=sin, name=name + "_bwd", **kw)
        return dx, dg, None, None

    op.defvjp(fwd, bwd)
    return op


@jax.custom_vjp
def modnorm(x, gain, shift, scale, mask):
    return _norm_call(x, gain, None, group=x.shape[1], quarter=0, cos=None, sin=None, shift=shift, scale=scale,
                      mask=mask, out_dtype=BF16, name="modnorm_fwd")


def _modnorm_fwd(x, gain, shift, scale, mask):
    return modnorm(x, gain, shift, scale, mask), (x, gain, shift, scale, mask)


def _modnorm_bwd(res, dy):
    x, gain, shift, scale, mask = res
    dx, dg, dsh, dsc = _norm_call(x, gain, dy, group=x.shape[1], quarter=0, cos=None, sin=None, shift=shift,
                                  scale=scale, mask=mask, out_dtype=BF16, name="modnorm_bwd")
    return dx, dg, dsh, dsc, None


modnorm.defvjp(_modnorm_fwd, _modnorm_bwd)


def _conv_call(ug, uv, cw, cb, da, *, seq, name):
    T, F = ug.shape
    bwd = da is not None
    tb = _pick(math.gcd(seq, T - seq), ROW_TILES)
    tf = _pick(F, (512, 256, 128))
    nlat, ntot, hpb = seq // tb, T // tb, tb // HALO
    n_ext = tb + 2 * HALO

    def body(*refs):
        if bwd:
            (g_ref, gp_ref, gn_ref, v_ref, vp_ref, vn_ref, d_ref, dp_ref, dn_ref, w_ref, b_ref,
             dg_ref, dv_ref, dw_ref, db_ref) = refs
        else:
            g_ref, gp_ref, gn_ref, v_ref, w_ref, b_ref, o_ref = refs
        i = pl.program_id(1)
        first = jnp.logical_or(i == 0, i == nlat)
        last = jnp.logical_or(i == nlat - 1, i == ntot - 1)
        w = w_ref[...]
        w0, w1, w2, b = w[0:1, :], w[1:2, :], w[2:3, :], b_ref[...]
        if not bwd:
            g = g_ref[...].astype(F32)
            hrow = lax.broadcasted_iota(jnp.int32, (HALO, tf), 0)
            prev = jnp.sum(jnp.where(hrow == HALO - 1, gp_ref[...].astype(F32), 0.0), axis=0, keepdims=True)
            nxt = jnp.sum(jnp.where(hrow == 0, gn_ref[...].astype(F32), 0.0), axis=0, keepdims=True)
            prev = jnp.where(first, 0.0, prev)
            nxt = jnp.where(last, 0.0, nxt)
            row = lax.broadcasted_iota(jnp.int32, g.shape, 0)
            gm1 = jnp.where(row == 0, prev, pltpu.roll(g, 1, 0))
            gp1 = jnp.where(row == tb - 1, nxt, pltpu.roll(g, tb - 1, 0))
            pre = w0 * gm1 + w1 * g + w2 * gp1 + b
            o_ref[...] = (pre * jax.nn.sigmoid(pre) * v_ref[...].astype(F32)).astype(o_ref.dtype)
        else:
            @pl.when(i == 0)
            def _():
                dw_ref[...] = jnp.zeros_like(dw_ref)
                db_ref[...] = jnp.zeros_like(db_ref)

            def ext(p_ref, m_ref, n_ref):
                return jnp.concatenate([p_ref[...].astype(F32), m_ref[...].astype(F32), n_ref[...].astype(F32)], axis=0)

            row = lax.broadcasted_iota(jnp.int32, (n_ext, tf), 0)
            valid = jnp.logical_and(jnp.logical_or(row >= HALO, jnp.logical_not(first)),
                                    jnp.logical_or(row < HALO + tb, jnp.logical_not(last)))
            ge = jnp.where(valid, ext(gp_ref, g_ref, gn_ref), 0.0)
            ve = ext(vp_ref, v_ref, vn_ref)
            de = ext(dp_ref, d_ref, dn_ref)
            gm1 = pltpu.roll(ge, 1, 0)
            gp1 = pltpu.roll(ge, n_ext - 1, 0)
            pre = w0 * gm1 + w1 * ge + w2 * gp1 + b
            sg = jax.nn.sigmoid(pre)
            dpre = jnp.where(valid, de * ve * (sg * (1.0 + pre * (1.0 - sg))), 0.0)
            dge = w0 * pltpu.roll(dpre, n_ext - 1, 0) + w1 * dpre + w2 * pltpu.roll(dpre, 1, 0)
            mid = slice(HALO, HALO + tb)
            dg_ref[...] = dge[mid].astype(dg_ref.dtype)
            dv_ref[...] = (de * pre * sg)[mid].astype(dv_ref.dtype)
            dw_ref[0:1, :] += jnp.sum((dpre * gm1)[mid], axis=0, keepdims=True)
            dw_ref[1:2, :] += jnp.sum((dpre * ge)[mid], axis=0, keepdims=True)
            dw_ref[2:3, :] += jnp.sum((dpre * gp1)[mid], axis=0, keepdims=True)
            db_ref[...] += jnp.sum(dpre[mid], axis=0, keepdims=True)

    main = pl.BlockSpec((tb, tf), lambda j, i: (i, j))
    prev = pl.BlockSpec((HALO, tf), lambda j, i: (jnp.maximum(i * hpb - 1, 0), j))
    nxt = pl.BlockSpec((HALO, tf), lambda j, i: (jnp.minimum((i + 1) * hpb, T // HALO - 1), j))
    wspec = pl.BlockSpec((3, tf), lambda j, i: (0, j))
    bspec = pl.BlockSpec((1, tf), lambda j, i: (0, j))
    if not bwd:
        in_specs, args = [main, prev, nxt, main, wspec, bspec], [ug, ug, ug, uv, cw, cb]
        out_shape, out_specs = jax.ShapeDtypeStruct((T, F), BF16), main
    else:
        in_specs = [main, prev, nxt] * 3 + [wspec, bspec]
        args = [ug, ug, ug, uv, uv, uv, da, da, da, cw, cb]
        out_shape = (jax.ShapeDtypeStruct((T, F), BF16), jax.ShapeDtypeStruct((T, F), BF16),
                     jax.ShapeDtypeStruct((3, F), F32), jax.ShapeDtypeStruct((1, F), F32))
        out_specs = (main, main, wspec, bspec)
    return pl.pallas_call(
        body, name=name, grid=(F // tf, T // tb), in_specs=in_specs, out_specs=out_specs, out_shape=out_shape,
        compiler_params=_params("parallel", "arbitrary" if bwd else "parallel"),
    )(*args)


def make_convgate(seq):
    @jax.custom_vjp
    def op(ug, uv, cw, cb):
        return _conv_call(ug, uv, cw, cb, None, seq=seq, name="conv_fwd")

    def fwd(ug, uv, cw, cb):
        return op(ug, uv, cw, cb), (ug, uv, cw, cb)

    def bwd(res, da):
        ug, uv, cw, cb = res
        return _conv_call(ug, uv, cw, cb, da, seq=seq, name="conv_bwd")

    op.defvjp(fwd, bwd)
    return op


def _attn_tiles(q_len, k_len):
    return _pick(q_len, (512, 256, 128)), _pick(k_len, (768, 512, 256, 128))


def _attn_specs(hm, G, d, dv, tq, tk, qoff, koff):
    if hm:
        q_spec = pl.BlockSpec((None, tq, d), lambda h, g, i, j: (h * G + g, qoff + i, 0))
        k_spec = pl.BlockSpec((None, tk, d), lambda h, g, i, j: (h, koff + j, 0))
    else:
        q_spec = pl.BlockSpec((tq, d), lambda h, g, i, j: (qoff + i, h * G + g))
        k_spec = pl.BlockSpec((tk, d), lambda h, g, i, j: (koff + j, h))
    v_spec = pl.BlockSpec((tk, dv), lambda h, g, i, j: (koff + j, h))
    o_spec = pl.BlockSpec((tq, dv), lambda h, g, i, j: (i, h * G + g))
    l_spec = pl.BlockSpec((None, tq, 1), lambda h, g, i, j: (h * G + g, i, 0))
    return q_spec, k_spec, v_spec, o_spec, l_spec


def _attn_fwd_call(q, k, v, *, hm, Hk, G, d, dv, scale, q0, q_len, k0, k_len, name):
    tq, tk = _attn_tiles(q_len, k_len)
    nq, nk, H = q_len // tq, k_len // tk, Hk * G

    def body(q_ref, k_ref, v_ref, o_ref, lse_ref, m_s, l_s, acc_s):
        kj = pl.program_id(3)

        @pl.when(kj == 0)
        def _():
            m_s[...] = jnp.full_like(m_s, -jnp.inf)
            l_s[...] = jnp.zeros_like(l_s)
            acc_s[...] = jnp.zeros_like(acc_s)

        s = lax.dot_general(q_ref[...], k_ref[...], (((1,), (1,)), ((), ())), preferred_element_type=F32) * scale
        m_prev = m_s[...]
        m_new = jnp.maximum(m_prev, jnp.max(s, axis=-1, keepdims=True))
        alpha = jnp.exp(m_prev - m_new)
        p = jnp.exp(s - m_new)
        l_s[...] = alpha * l_s[...] + jnp.sum(p, axis=-1, keepdims=True)
        acc_s[...] = alpha * acc_s[...] + jnp.dot(p.astype(BF16), v_ref[...], preferred_element_type=F32)
        m_s[...] = m_new

        @pl.when(kj == nk - 1)
        def _():
            o_ref[...] = (acc_s[...] / l_s[...]).astype(o_ref.dtype)
            lse_ref[...] = m_s[...] + jnp.log(l_s[...])

    q_spec, k_spec, v_spec, o_spec, l_spec = _attn_specs(hm, G, d, dv, tq, tk, q0 // tq, k0 // tk)
    return pl.pallas_call(
        body, name=name, grid=(Hk, G, nq, nk), in_specs=[q_spec, k_spec, v_spec], out_specs=(o_spec, l_spec),
        out_shape=(jax.ShapeDtypeStruct((q_len, H * dv), BF16), jax.ShapeDtypeStruct((H, q_len, 1), F32)),
        scratch_shapes=[pltpu.VMEM((tq, 1), F32), pltpu.VMEM((tq, 1), F32), pltpu.VMEM((tq, dv), F32)],
        compiler_params=_params("parallel", "parallel", "parallel", "arbitrary"),
    )(q, k, v)


def _attn_bwd_call(q, k, v, o, lse, do, *, hm, Hk, G, d, dv, scale, q0, q_len, k0, k_len, name):
    tq, tk = _attn_tiles(q_len, k_len)
    nq, nk, H = q_len // tq, k_len // tk, Hk * G

    def body(q_ref, k_ref, v_ref, o_ref, do_ref, lse_ref, dq_ref, dk_ref, dv_ref, dq_s, delta_s):
        g, qi, kj = pl.program_id(1), pl.program_id(2), pl.program_id(3)

        @pl.when(jnp.logical_and(jnp.logical_and(g == 0, qi == 0), kj == 0))
        def _():
            dk_ref[...] = jnp.zeros_like(dk_ref)
            dv_ref[...] = jnp.zeros_like(dv_ref)

        do_v = do_ref[...]

        @pl.when(kj == 0)
        def _():
            delta_s[...] = jnp.sum(do_v.astype(F32) * o_ref[...].astype(F32), axis=-1, keepdims=True)
            dq_s[...] = jnp.zeros_like(dq_s)

        q_v, k_v = q_ref[...], k_ref[...]
        s = lax.dot_general(q_v, k_v, (((1,), (1,)), ((), ())), preferred_element_type=F32) * scale
        p = jnp.exp(s - lse_ref[...])
        rows = pl.ds(pl.multiple_of(kj * tk, tk), tk)
        dv_ref[rows, :] += lax.dot_general(p.astype(BF16), do_v, (((0,), (0,)), ((), ())), preferred_element_type=F32)
        dp = lax.dot_general(do_v, v_ref[...], (((1,), (1,)), ((), ())), preferred_element_type=F32)
        ds = (p * (dp - delta_s[...]) * scale).astype(BF16)
        dk_ref[rows, :] += lax.dot_general(ds, q_v, (((0,), (0,)), ((), ())), preferred_element_type=F32)
        dq_s[...] += jnp.dot(ds, k_v, preferred_element_type=F32)

        @pl.when(kj == nk - 1)
        def _():
            dq_ref[...] = dq_s[...].astype(dq_ref.dtype)

    q_spec, k_spec, v_spec, o_spec, l_spec = _attn_specs(hm, G, d, dv, tq, tk, q0 // tq, k0 // tk)
    if hm:
        dq_spec = pl.BlockSpec((None, tq, d), lambda h, g, i, j: (h * G + g, i, 0))
        dk_spec = pl.BlockSpec((None, k_len, d), lambda h, g, i, j: (h, 0, 0))
        dq_shape, dk_shape = (H, q_len, d), (Hk, k_len, d)
    else:
        dq_spec = pl.BlockSpec((tq, d), lambda h, g, i, j: (i, h * G + g))
        dk_spec = pl.BlockSpec((k_len, d), lambda h, g, i, j: (0, h))
        dq_shape, dk_shape = (q_len, H * d), (k_len, Hk * d)
    dv_spec = pl.BlockSpec((k_len, dv), lambda h, g, i, j: (0, h))
    return pl.pallas_call(
        body, name=name, grid=(Hk, G, nq, nk),
        in_specs=[q_spec, k_spec, v_spec, o_spec, o_spec, l_spec], out_specs=(dq_spec, dk_spec, dv_spec),
        out_shape=(jax.ShapeDtypeStruct(dq_shape, BF16), jax.ShapeDtypeStruct(dk_shape, F32),
                   jax.ShapeDtypeStruct((k_len, Hk * dv), F32)),
        scratch_shapes=[pltpu.VMEM((tq, d), F32), pltpu.VMEM((tq, 1), F32)],
        compiler_params=_params("parallel", "arbitrary", "arbitrary", "arbitrary"),
    )(q, k, v, o, do, lse)


def make_attention(*, hm, Hk, G, d, dv, scale, seq, total, name):
    kw = dict(hm=hm, Hk=Hk, G=G, d=d, dv=dv, scale=scale)
    lat = dict(q0=0, q_len=seq, k0=0, k_len=total)
    ctx = dict(q0=seq, q_len=total - seq, k0=seq, k_len=total - seq)

    def run(q, k, v):
        o1, l1 = _attn_fwd_call(q, k, v, name=name + "_fwd_lat", **kw, **lat)
        o2, l2 = _attn_fwd_call(q, k, v, name=name + "_fwd_ctx", **kw, **ctx)
        return jnp.concatenate([o1, o2], axis=0), (l1, l2)

    @jax.custom_vjp
    def op(q, k, v):
        return run(q, k, v)[0]

    def fwd(q, k, v):
        o, lses = run(q, k, v)
        return o, (q, k, v, o, lses)

    def bwd(res, do):
        q, k, v, o, (l1, l2) = res
        dq1, dk1, dv1 = _attn_bwd_call(q, k, v, o[:seq], l1, do[:seq], name=name + "_bwd_lat", **kw, **lat)
        dq2, dk2, dv2 = _attn_bwd_call(q, k, v, o[seq:], l2, do[seq:], name=name + "_bwd_ctx", **kw, **ctx)
        tok = 1 if hm else 0
        dq = jnp.concatenate([dq1, dq2], axis=tok)
        pad = [(0, 0)] * dk1.ndim
        pad[tok] = (seq, 0)
        dk = dk1 + jnp.pad(dk2, pad)
        dvv = dv1 + jnp.pad(dv2, [(seq, 0), (0, 0)])
        return dq.astype(q.dtype), dk.astype(k.dtype), dvv.astype(v.dtype)

    op.defvjp(fwd, bwd)
    return op


def _loss_call(xf, target):
    S, D = target.shape
    tb = _pick(S, ROW_TILES)

    def body(y_ref, t_ref, dy_ref, acc_ref):
        @pl.when(pl.program_id(0) == 0)
        def _():
            acc_ref[...] = jnp.zeros_like(acc_ref)

        e = y_ref[...] - t_ref[...]
        dy_ref[...] = e * (1.0 / D)
        acc_ref[...] += jnp.sum(e * e, axis=0, keepdims=True)

    row = pl.BlockSpec((tb, D), lambda i: (i, 0))
    return pl.pallas_call(
        body, name="loss", grid=(S // tb,), in_specs=[row, row],
        out_specs=(row, pl.BlockSpec((1, D), lambda i: (0, 0))),
        out_shape=(jax.ShapeDtypeStruct((S, D), F32), jax.ShapeDtypeStruct((1, D), F32)),
        compiler_params=_params("arbitrary"),
    )(xf, target)


def _adamw_call(w, g, m, v, name):
    R, N = w.shape
    tb = _pick(R, tuple(t for t in (512, 256, 128, 64, 32, 16, 8) if t * N * 4 <= (1 << 20)) or (8,))
    bc1, bc2 = 1.0 - ADAM_B1 ** ADAM_STEP, 1.0 - ADAM_B2 ** ADAM_STEP

    def body(w_ref, g_ref, m_ref, v_ref, d_ref, nm_ref, nv_ref):
        gv = g_ref[...]
        nm = ADAM_B1 * m_ref[...] + (1.0 - ADAM_B1) * gv
        nv = ADAM_B2 * v_ref[...] + (1.0 - ADAM_B2) * (gv * gv)
        d_ref[...] = -ADAM_LR * ((nm / bc1) / (jnp.sqrt(nv / bc2) + ADAM_EPS) + ADAM_WD * w_ref[...])
        nm_ref[...] = nm
        nv_ref[...] = nv

    spec = pl.BlockSpec((tb, N), lambda i: (i, 0))
    shp = jax.ShapeDtypeStruct((R, N), F32)
    return pl.pallas_call(
        body, name=name, grid=(R // tb,), in_specs=[spec] * 4, out_specs=(spec,) * 3, out_shape=(shp,) * 3,
        compiler_params=_params("parallel"),
    )(w, g, m, v)


def _adamw(w, g, m, v, name="adamw"):
    shape = w.shape
    two = (lambda a: a.reshape(-1, shape[-1])) if w.ndim >= 2 else (lambda a: a.reshape(1, -1))
    return tuple(r.reshape(shape) for r in _adamw_call(two(w), two(g), two(m), two(v), name))


def _me():
    return lax.axis_index("x"), lax.axis_index("y"), lax.axis_index("c")


def _gather8(x, *, reduce, name):
    R, W = x.shape

    def body(x_ref, out_ref, *scratch):
        if reduce:
            buf, send_sems, recv_sems, local_sem = scratch
        else:
            buf = out_ref
            send_sems, recv_sems, local_sem = scratch
        x, y, c = _me()
        me, sibling = (x, y, c), (x, y, 1 - c)
        chips = [(1 - x, y), (x, 1 - y), (1 - x, 1 - y)]

        def rows(px, py, pc):
            return buf.at[pl.ds((4 * px + 2 * py + pc) * R, R), :]

        def copy(k, block, to, src=None):
            return pltpu.make_async_remote_copy(
                src_ref=rows(*block) if src is None else src, dst_ref=rows(*block),
                send_sem=send_sems.at[k], recv_sem=recv_sems.at[k], device_id=to, device_id_type=MESH)

        mine = pltpu.make_async_copy(x_ref, rows(*me), local_sem)
        mine.start()
        first = [copy(0, me, sibling, src=x_ref)]
        first += [copy(1 + j, me, (*chip, c), src=x_ref) for j, chip in enumerate(chips)]
        for cp in first:
            cp.start()
        passed = [copy(4 + j, (*chip, c), sibling) for j, chip in enumerate(chips)]
        for j, chip in enumerate(chips):
            copy(1 + j, (*chip, c), me).wait_recv()
            passed[j].start()
        copy(0, sibling, me).wait_recv()
        for j, chip in enumerate(chips):
            copy(4 + j, (*chip, 1 - c), me).wait_recv()
        for cp in first + passed:
            cp.wait_send()
        mine.wait()
        if reduce:
            acc = buf[0:R, :]
            for e in range(1, 8):
                acc = acc + buf[e * R:(e + 1) * R, :]
            out_ref[...] = acc

    sems = [pltpu.SemaphoreType.DMA((7,)), pltpu.SemaphoreType.DMA((7,)), pltpu.SemaphoreType.DMA]
    return pl.pallas_call(
        body, name=name,
        out_shape=jax.ShapeDtypeStruct((R if reduce else 8 * R, W), F32),
        in_specs=[pl.BlockSpec(memory_space=pltpu.VMEM)], out_specs=pl.BlockSpec(memory_space=pltpu.VMEM),
        scratch_shapes=([pltpu.VMEM((8 * R, W), F32)] if reduce else []) + sems,
        compiler_params=pltpu.CompilerParams(vmem_limit_bytes=VMEM_LIMIT),
    )(x)


def _small_exchange(flat, *, reduce, name):
    n = flat.shape[0]
    W = PACK_W if n >= 8 * PACK_W else LANES
    R = -(-n // (8 * W)) * 8
    x = jnp.pad(flat, (0, R * W - n)).reshape(R, W)
    out = _gather8(x, reduce=reduce, name=name)
    if reduce:
        return out.reshape(-1)[:n]
    return out.reshape(8, R * W)[:, :n]


HBM_SPEC = pl.BlockSpec(memory_space=pl.ANY)


def _chip_gather(mine):
    _, R, W = mine.shape

    def body(m_ref, out_ref, send_sems, recv_sems, local_sems):
        x, y, c = _me()
        sibling = (x, y, 1 - c)
        chips = [(1 - x, y), (x, 1 - y), (1 - x, 1 - y)]

        def slot(px, py, half):
            return out_ref.at[half, 2 * px + py]

        def copy(k, src, dst, to):
            return pltpu.make_async_remote_copy(src_ref=src, dst_ref=dst, send_sem=send_sems.at[k],
                                                recv_sem=recv_sems.at[k], device_id=to, device_id_type=MESH)

        local = [pltpu.make_async_copy(m_ref.at[hf], slot(x, y, hf), local_sems.at[hf]) for hf in range(2)]
        for cp in local:
            cp.start()
        first = [copy(j, m_ref.at[c], slot(x, y, c), (*chip, c)) for j, chip in enumerate(chips)]
        for cp in first:
            cp.start()
        passed = [copy(3 + j, slot(*chip, c), slot(*chip, c), sibling) for j, chip in enumerate(chips)]
        for j, chip in enumerate(chips):
            copy(j, m_ref.at[c], slot(*chip, c), (*chip, c)).wait_recv()
            passed[j].start()
        for j, chip in enumerate(chips):
            copy(3 + j, slot(*chip, 1 - c), slot(*chip, 1 - c), sibling).wait_recv()
        for cp in first + passed:
            cp.wait_send()
        for cp in local:
            cp.wait()

    return pl.pallas_call(
        body, name="chip_gather", out_shape=jax.ShapeDtypeStruct((2, 4, R, W), mine.dtype),
        in_specs=[HBM_SPEC], out_specs=HBM_SPEC,
        scratch_shapes=[pltpu.SemaphoreType.DMA((6,)), pltpu.SemaphoreType.DMA((6,)), pltpu.SemaphoreType.DMA((2,))],
    )(mine)


def _pair_swap(buf, *, pick_other_half, name):
    shape = buf.shape[1:] if pick_other_half else buf.shape

    def body(b_ref, out_ref, send_sem, recv_sem):
        x, y, c = _me()
        src = b_ref.at[1 - c] if pick_other_half else b_ref
        cp = pltpu.make_async_remote_copy(src_ref=src, dst_ref=out_ref, send_sem=send_sem, recv_sem=recv_sem,
                                          device_id=(x, y, 1 - c), device_id_type=MESH)
        cp.start()
        cp.wait()

    return pl.pallas_call(
        body, name=name, out_shape=jax.ShapeDtypeStruct(shape, buf.dtype), in_specs=[HBM_SPEC], out_specs=HBM_SPEC,
        scratch_shapes=[pltpu.SemaphoreType.DMA, pltpu.SemaphoreType.DMA],
    )(buf)


def _chip_exchange(s):
    def body(s_ref, out_ref, send_sems, recv_sems, local_sem):
        x, y, c = _me()
        q = 2 * x + y
        chips = [(1 - x, y), (x, 1 - y), (1 - x, 1 - y)]

        def copy(j, chip):
            return pltpu.make_async_remote_copy(
                src_ref=s_ref.at[2 * chip[0] + chip[1]], dst_ref=out_ref.at[q], send_sem=send_sems.at[j],
                recv_sem=recv_sems.at[j], device_id=(*chip, c), device_id_type=MESH)

        local = pltpu.make_async_copy(s_ref.at[q], out_ref.at[q], local_sem)
        local.start()
        sends = [copy(j, chip) for j, chip in enumerate(chips)]
        for cp in sends:
            cp.start()
        for j, chip in enumerate(chips):
            pltpu.make_async_remote_copy(
                src_ref=s_ref.at[q], dst_ref=out_ref.at[2 * chip[0] + chip[1]], send_sem=send_sems.at[j],
                recv_sem=recv_sems.at[j], device_id=(*chip, c), device_id_type=MESH).wait_recv()
        for cp in sends:
            cp.wait_send()
        local.wait()

    return pl.pallas_call(
        body, name="chip_exchange", out_shape=jax.ShapeDtypeStruct(s.shape, s.dtype), in_specs=[HBM_SPEC],
        out_specs=HBM_SPEC,
        scratch_shapes=[pltpu.SemaphoreType.DMA((3,)), pltpu.SemaphoreType.DMA((3,)), pltpu.SemaphoreType.DMA],
    )(s)


def _pair_sum(p, got, c):
    _, _, R, W = p.shape
    tb = _pick(R, ROW_TILES)

    def body(c_ref, p_ref, g_ref, o_ref):
        o_ref[...] = (p_ref[...].astype(F32) + g_ref[...].astype(F32)).astype(o_ref.dtype)

    return pl.pallas_call(
        body, name="pair_sum",
        grid_spec=pltpu.PrefetchScalarGridSpec(
            num_scalar_prefetch=1, grid=(4, R // tb),
            in_specs=[pl.BlockSpec((None, None, tb, W), lambda s, i, cr: (cr[0], s, i, 0)),
                      pl.BlockSpec((None, tb, W), lambda s, i, cr: (s, i, 0))],
            out_specs=pl.BlockSpec((None, tb, W), lambda s, i, cr: (s, i, 0))),
        out_shape=jax.ShapeDtypeStruct((4, R, W), BF16),
        compiler_params=_params("parallel", "parallel"),
    )(jnp.reshape(c, (1,)).astype(jnp.int32), p, got)


def _sum4(r):
    _, R, W = r.shape
    tb = _pick(R, ROW_TILES)

    def body(r_ref, o_ref):
        acc = r_ref[0].astype(F32)
        for a in range(1, 4):
            acc = acc + r_ref[a].astype(F32)
        o_ref[...] = acc

    return pl.pallas_call(
        body, name="sum4", grid=(R // tb,), in_specs=[pl.BlockSpec((4, tb, W), lambda i: (0, i, 0))],
        out_specs=pl.BlockSpec((tb, W), lambda i: (i, 0)), out_shape=jax.ShapeDtypeStruct((R, W), F32),
        compiler_params=_params("parallel"),
    )(r)


BIG = (("mla_w_dq", 1), ("mla_w_uq", 2), ("mla_w_dkv", 1), ("mla_w_ukv", 2), ("mla_w_o", 1),
       ("gqa_w_q", 1), ("gqa_w_kv", 1), ("gqa_w_o", 1), ("ffn_w_up", 2), ("ffn_w_down", 1))


def _pack_rows(n):
    return -(-n // (2 * PACK_W * ROW_TILES[0])) * ROW_TILES[0]


def _to_shards(full, axis):
    L, K, N = full.shape
    if axis == 1:
        return full.reshape(L, 4, K // 4, N).transpose(1, 0, 2, 3).reshape(4, -1)
    return full.reshape(L, K, 4, N // 4).transpose(2, 0, 1, 3).reshape(4, -1)


def _from_shards(flat4, shard_shape, axis):
    L, K, N = shard_shape
    a = flat4.reshape(4, L, K, N)
    if axis == 1:
        return a.transpose(1, 0, 2, 3).reshape(L, 4 * K, N)
    return a.transpose(1, 2, 0, 3).reshape(L, K, 4 * N)


def _rope_tables(seq, total, rot_dim):
    t = jnp.arange(seq, dtype=jnp.int32)
    rows, cols = t // GRID_W, t % GRID_W
    axis_dim = rot_dim // 2
    inv = jnp.power(ROPE_BASE, -jnp.arange(0, axis_dim, 2, dtype=F32) / axis_dim)
    ang_r = rows.astype(F32)[:, None] * inv
    ang_c = cols.astype(F32)[:, None] * inv
    ang = jnp.concatenate([ang_r, ang_r, ang_c, ang_c], axis=-1)
    cos = jnp.concatenate([jnp.cos(ang), jnp.ones((total - seq, rot_dim), F32)], axis=0)
    sin = jnp.concatenate([jnp.sin(ang), jnp.zeros((total - seq, rot_dim), F32)], axis=0)
    rep = LANES // rot_dim
    return jnp.tile(cos, (1, rep)), jnp.tile(sin, (1, rep))


def _forward(X, mod, P, *, seq):
    T, D = X.shape
    depth = mod.shape[0]
    mask = (jnp.arange(T) >= seq).astype(F32)[:, None]
    rope_m = _rope_tables(seq, T, ROPE_D)
    rope_g = _rope_tables(seq, T, HEAD)
    Hm = P["mla_w_uq"].shape[-1] // (NOPE + ROPE_D)
    Hg = P["gqa_w_q"].shape[-1] // HEAD
    Hkv = P["gqa_w_kv"].shape[-1] // (2 * HEAD)
    kv_rank = P["mla_w_dkv"].shape[-1] - ROPE_D
    dff = P["ffn_conv_b"].shape[-1]

    norm_row = make_norm(P["mla_w_dq"].shape[-1], name="norm_rank")
    norm_kv = make_norm(kv_rank, name="norm_kvrank")
    norm_head = make_norm(LANES, name="norm_head")
    norm_pe = make_norm(ROPE_D, ROPE_D // 4, name="norm_pe")
    norm_gqa = make_norm(HEAD, HEAD // 4, name="norm_gqa")
    attn_mla = make_attention(hm=True, Hk=Hm, G=1, d=NOPE + ROPE_D, dv=VDIM, scale=1.0 / math.sqrt(NOPE + ROPE_D),
                              seq=seq, total=T, name="mla")
    attn_gqa = make_attention(hm=False, Hk=Hkv, G=Hg // Hkv, d=HEAD, dv=HEAD, scale=1.0 / math.sqrt(HEAD),
                              seq=seq, total=T, name="gqa")
    convgate = make_convgate(seq)
    row = lambda g: g.reshape(1, -1)
    twice = lambda g: jnp.concatenate([g, g]).reshape(1, -1)

    for i in range(depth):
        j = i // 2
        sh1, sc1, g1, sh2, sc2, g2 = [mod[i][:, k * D:(k + 1) * D] for k in range(N_MOD)]
        h = modnorm(X, row(P["norm_mix"][i]), sh1, sc1, mask)
        if i % 2 == 0:
            w_uq = P["mla_w_uq"][j].reshape(-1, Hm, NOPE + ROPE_D)
            w_ukv = P["mla_w_ukv"][j].reshape(-1, Hm, NOPE + VDIM)
            cq = norm_row(linear(h, P["mla_w_dq"][j]), row(P["mla_g_dq"][j]), None, None)
            qn = norm_head(linear(cq, w_uq[:, :, :NOPE].reshape(-1, Hm * NOPE)), row(P["mla_g_q_nope"][j]), None, None)
            qp = norm_pe(linear(cq, w_uq[:, :, NOPE:].reshape(-1, Hm * ROPE_D)), twice(P["mla_g_q_pe"][j]), *rope_m)
            kva = linear(h, P["mla_w_dkv"][j])
            ckv = norm_kv(kva[:, :kv_rank], row(P["mla_g_dkv"][j]), None, None)
            kp_raw = kva[:, kv_rank:]
            kp = norm_pe(jnp.concatenate([kp_raw, kp_raw], axis=-1), twice(P["mla_g_k_pe"][j]), *rope_m)[:, :ROPE_D]
            kn = norm_head(linear(ckv, w_ukv[:, :, :NOPE].reshape(-1, Hm * NOPE)), row(P["mla_g_k_nope"][j]), None, None)
            v = linear_bf16(ckv, w_ukv[:, :, NOPE:].reshape(-1, Hm * VDIM))
            q_hm = jnp.concatenate([qn.reshape(T, Hm, NOPE), qp.reshape(T, Hm, ROPE_D)], axis=-1).transpose(1, 0, 2)
            k_hm = jnp.concatenate([kn.reshape(T, Hm, NOPE), jnp.broadcast_to(kp[:, None, :], (T, Hm, ROPE_D))],
                                   axis=-1).transpose(1, 0, 2)
            o = attn_mla(q_hm, k_hm, v)
            X = linear_gated(o, P["mla_w_o"][j], X, g1, mask)
        else:
            w_kv = P["gqa_w_kv"][j]
            q = norm_gqa(linear(h, P["gqa_w_q"][j]), row(P["gqa_g_q"][j]), *rope_g)
            k = norm_gqa(linear(h, w_kv[:, :Hkv * HEAD]), row(P["gqa_g_k"][j]), *rope_g)
            v = linear_bf16(h, w_kv[:, Hkv * HEAD:])
            o = attn_gqa(q, k, v)
            X = linear_gated(o, P["gqa_w_o"][j], X, g1, mask)
        h2 = modnorm(X, row(P["norm_ffn"][i]), sh2, sc2, mask)
        w_up = P["ffn_w_up"][i]
        ug = linear_bf16(h2, w_up[:, :dff])
        uv = linear_bf16(h2, w_up[:, dff:])
        a = convgate(ug, uv, P["ffn_conv_w"][i], row(P["ffn_conv_b"][i]))
        X = linear_gated(a, P["ffn_w_down"][i], X, g2, mask)
    return X


SMALL = ("norm_mix", "norm_ffn", "mla_g_dq", "mla_g_q_nope", "mla_g_q_pe", "mla_g_dkv", "mla_g_k_pe",
         "mla_g_k_nope", "gqa_g_q", "gqa_g_k", "ffn_conv_w", "ffn_conv_b")
WEIGHTS = ("c_ctx", "w_mod", "b_mod", "norm_mix", "norm_ffn", "mla_w_dq", "mla_g_dq", "mla_w_uq", "mla_g_q_nope",
           "mla_g_q_pe", "mla_w_dkv", "mla_g_dkv", "mla_g_k_pe", "mla_w_ukv", "mla_g_k_nope", "mla_w_o", "gqa_w_q",
           "gqa_g_q", "gqa_w_kv", "gqa_g_k", "gqa_w_o", "ffn_w_up", "ffn_conv_w", "ffn_conv_b", "ffn_w_down")


def _step(A):
    x, y, c = _me()
    q = 2 * x + y
    dev = 4 * x + 2 * y + c
    seq, D = A["x"].shape[1], A["x"].shape[2]
    depth = A["w_mod"].shape[0]
    X0 = jnp.concatenate([A["x"][0], A["ctx"][0]], axis=0)

    shard_n = [math.prod(A[n].shape) for n, _ in BIG]
    n_tot = sum(shard_n)
    R = _pack_rows(n_tot)
    flat = jnp.concatenate([A[n].astype(BF16).reshape(-1) for n, _ in BIG])
    mine = jnp.pad(flat, (0, 2 * R * PACK_W - n_tot)).reshape(2, R, PACK_W)
    allw = _chip_gather(mine).transpose(1, 0, 2, 3).reshape(4, -1)
    P, off = {}, 0
    for (n, axis), cnt in zip(BIG, shard_n):
        P[n] = _from_shards(allw[:, off:off + cnt], A[n].shape, axis)
        off += cnt

    c_all = _small_exchange(A["c"].reshape(-1), reduce=False, name="gather_c")
    cw = _small_exchange(A["ffn_conv_w"].reshape(-1), reduce=False, name="gather_convw")
    cw = cw.reshape(4, 2, depth, 3, -1)[:, 0].transpose(1, 2, 0, 3).reshape(depth, 3, -1)
    cvec = jnp.concatenate([c_all, jnp.broadcast_to(A["c_ctx"][None, :], (8, D))], axis=0)
    act = jax.nn.silu(cvec)
    ncol = A["w_mod"].shape[-1]
    modcols = jnp.stack([_mm(act, A["w_mod"][i], name="mod_fwd") for i in range(depth)])
    mod_all = _small_exchange(modcols.reshape(-1), reduce=False, name="gather_mod")
    mod_all = mod_all.reshape(4, 2, depth, 16, ncol)[:, 0].transpose(1, 2, 0, 3).reshape(depth, 16, 4 * ncol)
    mod_all = mod_all + A["b_mod"][:, None, :]
    mod = jnp.stack([lax.dynamic_index_in_dim(mod_all, dev, 1, keepdims=False), mod_all[:, 8]], axis=1)

    for n in SMALL:
        P[n] = cw if n == "ffn_conv_w" else A[n]

    XF, vjp = jax.vjp(functools.partial(_forward, seq=seq), X0, mod, P)
    dy, sq = _loss_call(XF, A["loss_target"][0])
    loss = lax.psum(0.5 / D * jnp.sum(sq), ("x", "y", "c"))
    dX0, dmod, dP = vjp(jnp.concatenate([dy, jnp.zeros((X0.shape[0] - seq, D), F32)], axis=0))
    grad_x = dX0[:seq][None]

    G = {}
    dmod_all = _small_exchange(dmod.reshape(-1), reduce=False, name="gather_dmod").reshape(8, depth, 2, -1)
    drows = jnp.concatenate([dmod_all[:, :, 0], dmod_all[:, :, 1]], axis=0).transpose(1, 0, 2)
    G["b_mod"] = jnp.sum(drows, axis=1)
    dcols = lax.dynamic_slice_in_dim(drows, q * ncol, ncol, axis=2)
    G["w_mod"] = jnp.stack([_mm(act, dcols[i], ta=True, name="mod_dw") for i in range(depth)])
    dact = sum(_mm(dcols[i], A["w_mod"][i], tb=True, name="mod_dact") for i in range(depth))
    dcc_part = 0.5 * jnp.sum(dact[8:], axis=0)

    small_list = [dP[n].astype(F32).reshape(-1) for n in SMALL] + [dcc_part]
    sizes = [a.shape[0] for a in small_list]
    tot = _small_exchange(jnp.concatenate(small_list), reduce=True, name="reduce_small")
    off = 0
    for n, cnt in zip(SMALL + ("c_ctx",), sizes):
        G[n] = tot[off:off + cnt].reshape(P[n].shape if n != "c_ctx" else (D,))
        off += cnt
    sg = jax.nn.sigmoid(A["c_ctx"])
    G["c_ctx"] = G["c_ctx"] * (sg * (1.0 + A["c_ctx"] * (1.0 - sg)))
    fcols = A["ffn_conv_w"].shape[-1]
    G["ffn_conv_w"] = lax.dynamic_slice_in_dim(G["ffn_conv_w"], q * fcols, fcols, axis=2)

    gflat = jnp.concatenate([_to_shards(dP[n], axis) for n, axis in BIG], axis=1)
    gp = jnp.pad(gflat, ((0, 0), (0, 2 * R * PACK_W - n_tot))).reshape(4, 2, R, PACK_W).transpose(1, 0, 2, 3)
    got = _pair_swap(gp, pick_other_half=True, name="pair_swap_grads")
    part = _chip_exchange(_pair_sum(gp, got, c))
    half = _sum4(part)
    other = _pair_swap(half, pick_other_half=False, name="pair_swap_sums")
    both = jnp.where(c == 0, jnp.stack([half, other]), jnp.stack([other, half])).reshape(-1)
    off = 0
    for (n, _), cnt in zip(BIG, shard_n):
        G[n] = both[off:off + cnt].reshape(A[n].shape)
        off += cnt

    delta, new_m, new_v = {}, {}, {}
    tiny = [n for n in WEIGHTS if A[n].size < (1 << 18)]
    cat = lambda pre: jnp.concatenate([(A[pre + n] if pre != "g" else G[n]).reshape(-1) for n in tiny])
    n_tiny = sum(A[n].size for n in tiny)
    rt = -(-n_tiny // (8 * PACK_W)) * 8
    shape2 = lambda a: jnp.pad(a, (0, rt * PACK_W - n_tiny)).reshape(rt, PACK_W)
    outs = _adamw_call(shape2(cat("")), shape2(cat("g")), shape2(cat("m_")), shape2(cat("v_")), "adamw_small")
    off = 0
    for n in tiny:
        cnt = A[n].size
        delta[n], new_m[n], new_v[n] = [o.reshape(-1)[off:off + cnt].reshape(A[n].shape) for o in outs]
        off += cnt
    for n in WEIGHTS:
        if n not in tiny:
            delta[n], new_m[n], new_v[n] = _adamw(A[n], G[n], A["m_" + n], A["v_" + n])
    return (loss, grad_x, *[G[n] for n in WEIGHTS], *[delta[n] for n in WEIGHTS],
            *[new_m[n] for n in WEIGHTS], *[new_v[n] for n in WEIGHTS])


def kernel(x, c, ctx, c_ctx, w_mod, b_mod, norm_mix, norm_ffn, mla_w_dq, mla_g_dq, mla_w_uq, mla_g_q_nope, mla_g_q_pe, mla_w_dkv, mla_g_dkv, mla_g_k_pe, mla_w_ukv, mla_g_k_nope, mla_w_o, gqa_w_q, gqa_g_q, gqa_w_kv, gqa_g_k, gqa_w_o, ffn_w_up, ffn_conv_w, ffn_conv_b, ffn_w_down, loss_target, m_c_ctx, m_w_mod, m_b_mod, m_norm_mix, m_norm_ffn, m_mla_w_dq, m_mla_g_dq, m_mla_w_uq, m_mla_g_q_nope, m_mla_g_q_pe, m_mla_w_dkv, m_mla_g_dkv, m_mla_g_k_pe, m_mla_w_ukv, m_mla_g_k_nope, m_mla_w_o, m_gqa_w_q, m_gqa_g_q, m_gqa_w_kv, m_gqa_g_k, m_gqa_w_o, m_ffn_w_up, m_ffn_conv_w, m_ffn_conv_b, m_ffn_w_down, v_c_ctx, v_w_mod, v_b_mod, v_norm_mix, v_norm_ffn, v_mla_w_dq, v_mla_g_dq, v_mla_w_uq, v_mla_g_q_nope, v_mla_g_q_pe, v_mla_w_dkv, v_mla_g_dkv, v_mla_g_k_pe, v_mla_w_ukv, v_mla_g_k_nope, v_mla_w_o, v_gqa_w_q, v_gqa_g_q, v_gqa_w_kv, v_gqa_g_k, v_gqa_w_o, v_ffn_w_up, v_ffn_conv_w, v_ffn_conv_b, v_ffn_w_down):
    return _step(dict(locals()))
```

```python
import functools
import math

import jax
import jax.numpy as jnp
from jax import lax
from jax.experimental import pallas as pl
from jax.experimental.pallas import tpu as pltpu

F32, BF16 = jnp.float32, jnp.bfloat16
MESH = pl.DeviceIdType.MESH

EPS = 1e-6
ROPE_BASE = 10000.0
GRID_W = 64
NOPE, ROPE_D, VDIM = 128, 64, 128
HEAD = 128
N_MOD = 6
LANES = 128
HALO = 16
VMEM_LIMIT = 56 * 1024 * 1024
PACK_W = 1024

ADAM_LR, ADAM_B1, ADAM_B2, ADAM_EPS, ADAM_WD, ADAM_STEP = 0.001, 0.9, 0.999, 1e-08, 0.01, 10

LANE_TILES = (1024, 768, 512, 256, 128)
K_RESIDENT = 2048
ROW_TILES = (256, 128, 64, 32, 16, 8)


def _pick(n, prefs):
    for p in prefs:
        if n % p == 0:
            return p
    return n


def _params(*sem):
    return pltpu.CompilerParams(dimension_semantics=sem, vmem_limit_bytes=VMEM_LIMIT)


def _mm(a, b, *, ta=False, tb=False, out_dtype=F32, resid=None, gate=None, mask=None, name="mm"):
    (K, M) = a.shape if ta else a.shape[::-1]
    N = b.shape[0] if tb else b.shape[1]
    assert (b.shape[1] if tb else b.shape[0]) == K
    tm, tn = _pick(M, LANE_TILES), _pick(N, LANE_TILES)
    tk = K if K <= K_RESIDENT else _pick(K, (1408,) + LANE_TILES)
    nk = K // tk
    fused = resid is not None
    dn = (((0 if ta else 1,), (1 if tb else 0,)), ((), ()))

    def body(*refs):
        refs = list(refs)
        acc_ref = refs.pop() if nk > 1 else None
        if fused:
            a_ref, b_ref, r_ref, g_ref, m_ref, o_ref, y_ref = refs
        else:
            a_ref, b_ref, o_ref = refs
        part = lax.dot_general(a_ref[...].astype(BF16), b_ref[...].astype(BF16), dn, preferred_element_type=F32)

        def finish(acc):
            if fused:
                g = jnp.where(m_ref[...] > 0.5, g_ref[1:2, :], g_ref[0:1, :])
                y_ref[...] = acc.astype(y_ref.dtype)
                o_ref[...] = r_ref[...] + g * acc
            else:
                o_ref[...] = acc.astype(o_ref.dtype)

        if nk == 1:
            finish(part)
        else:
            k = pl.program_id(2)

            @pl.when(k == 0)
            def _():
                acc_ref[...] = part

            @pl.when(k > 0)
            def _():
                acc_ref[...] += part

            @pl.when(k == nk - 1)
            def _():
                finish(acc_ref[...])

    a_spec = pl.BlockSpec((tk, tm), lambda i, j, k: (k, i)) if ta else pl.BlockSpec((tm, tk), lambda i, j, k: (i, k))
    b_spec = pl.BlockSpec((tn, tk), lambda i, j, k: (j, k)) if tb else pl.BlockSpec((tk, tn), lambda i, j, k: (k, j))
    o_spec = pl.BlockSpec((tm, tn), lambda i, j, k: (i, j))
    in_specs, args = [a_spec, b_spec], [a, b]
    out_shape, out_specs = jax.ShapeDtypeStruct((M, N), out_dtype), o_spec
    if fused:
        in_specs += [o_spec, pl.BlockSpec((2, tn), lambda i, j, k: (0, j)), pl.BlockSpec((tm, 1), lambda i, j, k: (i, 0))]
        args += [resid, gate, mask]
        out_shape = (jax.ShapeDtypeStruct((M, N), F32), jax.ShapeDtypeStruct((M, N), BF16))
        out_specs = (o_spec, o_spec)
    return pl.pallas_call(
        body, name=name, grid=(M // tm, N // tn, nk), in_specs=in_specs, out_specs=out_specs, out_shape=out_shape,
        scratch_shapes=[pltpu.VMEM((tm, tn), F32)] if nk > 1 else [],
        compiler_params=_params("parallel", "parallel", "arbitrary"),
    )(*args)


@jax.custom_vjp
def linear(a, w):
    return _mm(a, w, out_dtype=F32, name="lin_fwd")


def _linear_fwd(a, w):
    return linear(a, w), (a, w)


def _linear_bwd(res, dy):
    a, w = res
    return (_mm(dy, w, tb=True, out_dtype=a.dtype, name="lin_da"),
            _mm(a, dy, ta=True, out_dtype=w.dtype, name="lin_dw"))


linear.defvjp(_linear_fwd, _linear_bwd)


@jax.custom_vjp
def linear_bf16(a, w):
    return _mm(a, w, out_dtype=BF16, name="linb_fwd")


def _linear_bf16_fwd(a, w):
    return linear_bf16(a, w), (a, w)


linear_bf16.defvjp(_linear_bf16_fwd, _linear_bwd)


def _gate_bwd(dx, y, gate, mask):
    T, N = dx.shape
    tb = _pick(T, ROW_TILES)

    def body(dx_ref, y_ref, g_ref, m_ref, dy_ref, dg_ref):
        @pl.when(pl.program_id(0) == 0)
        def _():
            dg_ref[...] = jnp.zeros_like(dg_ref)

        d = dx_ref[...]
        ctx = m_ref[...] > 0.5
        dy_ref[...] = (jnp.where(ctx, g_ref[1:2, :], g_ref[0:1, :]) * d).astype(dy_ref.dtype)
        dyy = d * y_ref[...].astype(F32)
        dg_ref[0:1, :] += jnp.sum(jnp.where(ctx, 0.0, dyy), axis=0, keepdims=True)
        dg_ref[1:2, :] += jnp.sum(jnp.where(ctx, dyy, 0.0), axis=0, keepdims=True)

    row = pl.BlockSpec((tb, N), lambda i: (i, 0))
    small = pl.BlockSpec((2, N), lambda i: (0, 0))
    return pl.pallas_call(
        body, name="gate_bwd", grid=(T // tb,),
        in_specs=[row, row, small, pl.BlockSpec((tb, 1), lambda i: (i, 0))], out_specs=(row, small),
        out_shape=(jax.ShapeDtypeStruct((T, N), BF16), jax.ShapeDtypeStruct((2, N), F32)),
        compiler_params=_params("arbitrary"),
    )(dx, y, gate, mask)


def _make_linear_gated(at):
    @jax.custom_vjp
    def op(a, w, x, gate, mask):
        return _mm(a, w, ta=at, resid=x, gate=gate, mask=mask, name="ling_fwd")[0]

    def fwd(a, w, x, gate, mask):
        out, y = _mm(a, w, ta=at, resid=x, gate=gate, mask=mask, name="ling_fwd")
        return out, (a, w, y, gate, mask)

    def bwd(res, dx):
        a, w, y, gate, mask = res
        dy, dgate = _gate_bwd(dx, y, gate, mask)
        if at:
            da = _mm(w, dy, tb=True, out_dtype=a.dtype, name="ling_da_t")
            dw = _mm(a, dy, out_dtype=w.dtype, name="ling_dw_t")
        else:
            da = _mm(dy, w, tb=True, out_dtype=a.dtype, name="ling_da")
            dw = _mm(a, dy, ta=True, out_dtype=w.dtype, name="ling_dw")
        return da, dw, dx, dgate, None

    op.defvjp(fwd, bwd)
    return op


linear_gated = _make_linear_gated(False)
linear_gated_t = _make_linear_gated(True)


def _rope_rot(x, quarter):
    lane = lax.broadcasted_iota(jnp.int32, x.shape, 1)
    lo = (lane % (2 * quarter)) < quarter
    return jnp.where(lo, -pltpu.roll(x, LANES - quarter, 1), pltpu.roll(x, quarter, 1))


def _group_mean(v, group):
    if group == LANES:
        return jnp.mean(v, axis=-1, keepdims=True)
    lane = lax.broadcasted_iota(jnp.int32, v.shape, 1)
    lo = lane < group
    s_lo = jnp.sum(jnp.where(lo, v, 0.0), axis=-1, keepdims=True)
    s_hi = jnp.sum(jnp.where(lo, 0.0, v), axis=-1, keepdims=True)
    return jnp.where(lo, s_lo, s_hi) * (1.0 / group)


def _norm_call(x, gain, dy, *, group, quarter, cos, sin, shift, scale, mask, out_dtype, name):
    T, W = x.shape
    whole = group == W
    use_rope, use_mod, bwd = cos is not None, shift is not None, dy is not None
    assert not (whole and use_rope) and not (use_mod and not whole)
    tb = _pick(T, ROW_TILES)
    nchunk = W // LANES

    def body(*refs):
        refs = list(refs)
        x_ref, g_ref = refs.pop(0), refs.pop(0)
        dy_ref = refs.pop(0) if bwd else None
        cos_ref, sin_ref = (refs.pop(0), refs.pop(0)) if use_rope else (None, None)
        sh_ref, sc_ref, m_ref = (refs.pop(0), refs.pop(0), refs.pop(0)) if use_mod else (None, None, None)
        if not bwd:
            (o_ref,) = refs
        elif use_mod:
            dx_ref, dg_ref, dsh_ref, dsc_ref = refs
        else:
            dx_ref, dg_ref = refs

        if bwd:
            @pl.when(pl.program_id(0) == 0)
            def _():
                dg_ref[...] = jnp.zeros_like(dg_ref)
                if use_mod:
                    dsh_ref[...] = jnp.zeros_like(dsh_ref)
                    dsc_ref[...] = jnp.zeros_like(dsc_ref)

        if whole:
            xv = x_ref[...].astype(F32)
            r = lax.rsqrt(jnp.mean(xv * xv, axis=-1, keepdims=True) + EPS)
            xn = xv * r
            gain_v = g_ref[...]
            if use_mod:
                ctx = m_ref[...] > 0.5
                sc = jnp.where(ctx, sc_ref[1:2, :], sc_ref[0:1, :])
            if not bwd:
                y = xn * gain_v
                if use_mod:
                    y = y * (1.0 + sc) + jnp.where(ctx, sh_ref[1:2, :], sh_ref[0:1, :])
                o_ref[...] = y.astype(o_ref.dtype)
            else:
                d = dy_ref[...].astype(F32)
                if use_mod:
                    dyy = d * (xn * gain_v)
                    dsh_ref[0:1, :] += jnp.sum(jnp.where(ctx, 0.0, d), axis=0, keepdims=True)
                    dsh_ref[1:2, :] += jnp.sum(jnp.where(ctx, d, 0.0), axis=0, keepdims=True)
                    dsc_ref[0:1, :] += jnp.sum(jnp.where(ctx, 0.0, dyy), axis=0, keepdims=True)
                    dsc_ref[1:2, :] += jnp.sum(jnp.where(ctx, dyy, 0.0), axis=0, keepdims=True)
                    d = d * (1.0 + sc)
                dg_ref[...] += jnp.sum(d * xn, axis=0, keepdims=True)
                dxn = d * gain_v
                dx_ref[...] = (r * (dxn - xn * jnp.mean(dxn * xn, axis=-1, keepdims=True))).astype(dx_ref.dtype)
        else:
            gain_v = g_ref[...]
            if use_rope:
                cs, sn = cos_ref[...], sin_ref[...]
            for c in range(nchunk):
                cols = slice(c * LANES, (c + 1) * LANES)
                xv = x_ref[:, cols].astype(F32)
                r = lax.rsqrt(_group_mean(xv * xv, group) + EPS)
                xn = xv * r
                if not bwd:
                    y = xn * gain_v
                    if use_rope:
                        y = y * cs + _rope_rot(y, quarter) * sn
                    o_ref[:, cols] = y.astype(o_ref.dtype)
                else:
                    d = dy_ref[:, cols].astype(F32)
                    if use_rope:
                        d = d * cs - _rope_rot(d * sn, quarter)
                    dg_ref[...] += jnp.sum(d * xn, axis=0, keepdims=True)
                    dxn = d * gain_v
                    dx_ref[:, cols] = (r * (dxn - xn * _group_mean(dxn * xn, group))).astype(dx_ref.dtype)

    row = pl.BlockSpec((tb, W), lambda i: (i, 0))
    gw = W if whole else LANES
    gspec = pl.BlockSpec((1, gw), lambda i: (0, 0))
    in_specs, args = [row, gspec], [x, gain]
    if bwd:
        in_specs.append(row)
        args.append(dy)
    if use_rope:
        tab = pl.BlockSpec((tb, LANES), lambda i: (i, 0))
        in_specs += [tab, tab]
        args += [cos, sin]
    if use_mod:
        two = pl.BlockSpec((2, W), lambda i: (0, 0))
        in_specs += [two, two, pl.BlockSpec((tb, 1), lambda i: (i, 0))]
        args += [shift, scale, mask]
    if not bwd:
        out_shape, out_specs = jax.ShapeDtypeStruct((T, W), out_dtype), row
    else:
        out_shape = [jax.ShapeDtypeStruct((T, W), x.dtype), jax.ShapeDtypeStruct((1, gw), F32)]
        out_specs = [row, gspec]
        if use_mod:
            out_shape += [jax.ShapeDtypeStruct((2, W), F32)] * 2
            out_specs += [pl.BlockSpec((2, W), lambda i: (0, 0))] * 2
    return pl.pallas_call(
        body, name=name, grid=(T // tb,), in_specs=in_specs, out_specs=out_specs, out_shape=out_shape,
        compiler_params=_params("arbitrary" if bwd else "parallel"),
    )(*args)


def make_norm(group, quarter=0, name="norm"):
    kw = dict(group=group, quarter=quarter, shift=None, scale=None, mask=None, out_dtype=BF16)

    @jax.custom_vjp
    def op(x, gain, cos, sin):
        return _norm_call(x, gain, None, cos=cos, sin=sin, name=name + "_fwd", **kw)

    def fwd(x, gain, cos, sin):
        return op(x, gain, cos, sin), (x, gain, cos, sin)

    def bwd(res, dy):
        x, gain, cos, sin = res
        dx, dg = _norm_call(x, gain, dy, cos=cos, sin=sin, name=name + "_bwd", **kw)
        return dx, dg, None, None

    op.defvjp(fwd, bwd)
    return op


@jax.custom_vjp
def modnorm(x, gain, shift, scale, mask):
    return _norm_call(x, gain, None, group=x.shape[1], quarter=0, cos=None, sin=None, shift=shift, scale=scale,
                      mask=mask, out_dtype=BF16, name="modnorm_fwd")


def _modnorm_fwd(x, gain, shift, scale, mask):
    return modnorm(x, gain, shift, scale, mask), (x, gain, shift, scale, mask)


def _modnorm_bwd(res, dy):
    x, gain, shift, scale, mask = res
    dx, dg, dsh, dsc = _norm_call(x, gain, dy, group=x.shape[1], quarter=0, cos=None, sin=None, shift=shift,
                                  scale=scale, mask=mask, out_dtype=BF16, name="modnorm_bwd")
    return dx, dg, dsh, dsc, None


modnorm.defvjp(_modnorm_fwd, _modnorm_bwd)


def _conv_call(ug, uv, cw, cb, da, *, seq, name):
    T, F = ug.shape
    bwd = da is not None
    tb = _pick(math.gcd(seq, T - seq), ROW_TILES)
    tf = _pick(F, (512, 256, 128))
    nlat, ntot, hpb = seq // tb, T // tb, tb // HALO
    n_ext = tb + 2 * HALO

    def body(*refs):
        if bwd:
            (g_ref, gp_ref, gn_ref, v_ref, vp_ref, vn_ref, d_ref, dp_ref, dn_ref, w_ref, b_ref,
             dg_ref, dv_ref, dw_ref, db_ref) = refs
        else:
            g_ref, gp_ref, gn_ref, v_ref, w_ref, b_ref, o_ref = refs
        i = pl.program_id(1)
        first = jnp.logical_or(i == 0, i == nlat)
        last = jnp.logical_or(i == nlat - 1, i == ntot - 1)
        w = w_ref[...]
        w0, w1, w2, b = w[0:1, :], w[1:2, :], w[2:3, :], b_ref[...]
        if not bwd:
            g = g_ref[...].astype(F32)
            hrow = lax.broadcasted_iota(jnp.int32, (HALO, tf), 0)
            prev = jnp.sum(jnp.where(hrow == HALO - 1, gp_ref[...].astype(F32), 0.0), axis=0, keepdims=True)
            nxt = jnp.sum(jnp.where(hrow == 0, gn_ref[...].astype(F32), 0.0), axis=0, keepdims=True)
            prev = jnp.where(first, 0.0, prev)
            nxt = jnp.where(last, 0.0, nxt)
            row = lax.broadcasted_iota(jnp.int32, g.shape, 0)
            gm1 = jnp.where(row == 0, prev, pltpu.roll(g, 1, 0))
            gp1 = jnp.where(row == tb - 1, nxt, pltpu.roll(g, tb - 1, 0))
            pre = w0 * gm1 + w1 * g + w2 * gp1 + b
            o_ref[...] = (pre * jax.nn.sigmoid(pre) * v_ref[...].astype(F32)).astype(o_ref.dtype)
        else:
            @pl.when(i == 0)
            def _():
                dw_ref[...] = jnp.zeros_like(dw_ref)
                db_ref[...] = jnp.zeros_like(db_ref)

            def ext(p_ref, m_ref, n_ref):
                return jnp.concatenate([p_ref[...].astype(F32), m_ref[...].astype(F32), n_ref[...].astype(F32)], axis=0)

            row = lax.broadcasted_iota(jnp.int32, (n_ext, tf), 0)
            valid = jnp.logical_and(jnp.logical_or(row >= HALO, jnp.logical_not(first)),
                                    jnp.logical_or(row < HALO + tb, jnp.logical_not(last)))
            ge = jnp.where(valid, ext(gp_ref, g_ref, gn_ref), 0.0)
            ve = ext(vp_ref, v_ref, vn_ref)
            de = ext(dp_ref, d_ref, dn_ref)
            gm1 = pltpu.roll(ge, 1, 0)
            gp1 = pltpu.roll(ge, n_ext - 1, 0)
            pre = w0 * gm1 + w1 * ge + w2 * gp1 + b
            sg = jax.nn.sigmoid(pre)
            dpre = jnp.where(valid, de * ve * (sg * (1.0 + pre * (1.0 - sg))), 0.0)
            dge = w0 * pltpu.roll(dpre, n_ext - 1, 0) + w1 * dpre + w2 * pltpu.roll(dpre, 1, 0)
            mid = slice(HALO, HALO + tb)
            dg_ref[...] = dge[mid].astype(dg_ref.dtype)
            dv_ref[...] = (de * pre * sg)[mid].astype(dv_ref.dtype)
            dw_ref[0:1, :] += jnp.sum((dpre * gm1)[mid], axis=0, keepdims=True)
            dw_ref[1:2, :] += jnp.sum((dpre * ge)[mid], axis=0, keepdims=True)
            dw_ref[2:3, :] += jnp.sum((dpre * gp1)[mid], axis=0, keepdims=True)
            db_ref[...] += jnp.sum(dpre[mid], axis=0, keepdims=True)

    main = pl.BlockSpec((tb, tf), lambda j, i: (i, j))
    prev = pl.BlockSpec((HALO, tf), lambda j, i: (jnp.maximum(i * hpb - 1, 0), j))
    nxt = pl.BlockSpec((HALO, tf), lambda j, i: (jnp.minimum((i + 1) * hpb, T // HALO - 1), j))
    wspec = pl.BlockSpec((3, tf), lambda j, i: (0, j))
    bspec = pl.BlockSpec((1, tf), lambda j, i: (0, j))
    if not bwd:
        in_specs, args = [main, prev, nxt, main, wspec, bspec], [ug, ug, ug, uv, cw, cb]
        out_shape, out_specs = jax.ShapeDtypeStruct((T, F), BF16), main
    else:
        in_specs = [main, prev, nxt] * 3 + [wspec, bspec]
        args = [ug, ug, ug, uv, uv, uv, da, da, da, cw, cb]
        out_shape = (jax.ShapeDtypeStruct((T, F), BF16), jax.ShapeDtypeStruct((T, F), BF16),
                     jax.ShapeDtypeStruct((3, F), F32), jax.ShapeDtypeStruct((1, F), F32))
        out_specs = (main, main, wspec, bspec)
    return pl.pallas_call(
        body, name=name, grid=(F // tf, T // tb), in_specs=in_specs, out_specs=out_specs, out_shape=out_shape,
        compiler_params=_params("parallel", "arbitrary" if bwd else "parallel"),
    )(*args)


def make_convgate(seq):
    @jax.custom_vjp
    def op(ug, uv, cw, cb):
        return _conv_call(ug, uv, cw, cb, None, seq=seq, name="conv_fwd")

    def fwd(ug, uv, cw, cb):
        return op(ug, uv, cw, cb), (ug, uv, cw, cb)

    def bwd(res, da):
        ug, uv, cw, cb = res
        return _conv_call(ug, uv, cw, cb, da, seq=seq, name="conv_bwd")

    op.defvjp(fwd, bwd)
    return op


def _attn_tiles(q_len, k_len):
    return _pick(q_len, (512, 256, 128)), _pick(k_len, (768, 512, 256, 128))


def _rows_spec(hm, t, d, head, blk):
    if hm:
        return pl.BlockSpec((None, t, d), lambda h, g, i, j: (head(h, g), blk(i, j), 0))
    return pl.BlockSpec((t, d), lambda h, g, i, j: (blk(i, j), head(h, g)))


def _cols_spec(hm, t, d, head, blk):
    if hm:
        return pl.BlockSpec((None, d, t), lambda h, g, i, j: (head(h, g), 0, blk(i, j)))
    return pl.BlockSpec((d, t), lambda h, g, i, j: (head(h, g), blk(i, j)))


LOG2E = 1.4426950408889634


def _attn_fwd_call(qT, k, vT, prev, *, hm, Hk, G, d, dv, scale, q0, q_len, k0, k_len, total, name):
    tq, tk = _attn_tiles(q_len, k_len)
    nq, nk, H = q_len // tq, k_len // tk, Hk * G
    qoff, koff = q0 // tq, k0 // tk
    c2 = scale * LOG2E

    def body(*refs):
        qT_ref, k_ref, vT_ref = refs[:3]
        oT_ref, lse_ref, m_s, l_s, acc_s = refs[-5:]
        kj = pl.program_id(3)

        @pl.when(kj == 0)
        def _():
            m_s[...] = jnp.full_like(m_s, -jnp.inf)
            l_s[...] = jnp.zeros_like(l_s)
            acc_s[...] = jnp.zeros_like(acc_s)

        sT = jnp.dot(k_ref[...], qT_ref[...], preferred_element_type=F32)
        m_prev = m_s[...]
        m_new = jnp.maximum(m_prev, jnp.max(sT, axis=0, keepdims=True))
        alpha = jnp.exp2((m_prev - m_new) * c2)
        pT = jnp.exp2((sT - m_new) * c2)
        l_s[...] = alpha * l_s[...] + jnp.sum(pT, axis=0, keepdims=True)
        acc_s[...] = alpha * acc_s[...] + jnp.dot(vT_ref[...], pT.astype(BF16), preferred_element_type=F32)
        m_s[...] = m_new

        @pl.when(kj == nk - 1)
        def _():
            oT_ref[...] = (acc_s[...] / l_s[...]).astype(oT_ref.dtype)
            lse_ref[...] = m_s[...] * scale + jnp.log(l_s[...])

    qh, kh = (lambda h, g: h * G + g), (lambda h, g: h)
    qb, kb = (lambda i, j: qoff + i), (lambda i, j: koff + j)
    in_specs = [_cols_spec(hm, tq, d, qh, qb), _rows_spec(hm, tk, d, kh, kb), _cols_spec(False, tk, dv, kh, kb)]
    args, alias = [qT, k, vT], {}
    if prev is not None:
        in_specs += [HBM_SPEC, HBM_SPEC]
        args += list(prev)
        alias = {3: 0, 4: 1}
    return pl.pallas_call(
        body, name=name, grid=(Hk, G, nq, nk), in_specs=in_specs,
        out_specs=(_cols_spec(False, tq, dv, qh, qb), pl.BlockSpec((None, 1, tq), lambda h, g, i, j: (h * G + g, 0, qoff + i))),
        out_shape=(jax.ShapeDtypeStruct((H * dv, total), BF16), jax.ShapeDtypeStruct((H, 1, total), F32)),
        scratch_shapes=[pltpu.VMEM((1, tq), F32), pltpu.VMEM((1, tq), F32), pltpu.VMEM((dv, tq), F32)],
        input_output_aliases=alias,
        compiler_params=_params("parallel", "parallel", "parallel", "arbitrary"),
    )(*args)


def _attn_bwd_call(qT, q, k, kT, v, doT, do, oT, lse, prev, *, hm, Hk, G, d, dv, scale, q0, q_len, k0, k_len, total, name):
    tq, tk = _attn_tiles(q_len, k_len)
    nq, nk, H = q_len // tq, k_len // tk, Hk * G
    qoff, koff = q0 // tq, k0 // tk
    c2 = scale * LOG2E
    acc_in = prev is not None

    def body(*refs):
        qT_ref, q_ref, k_ref, kT_ref, v_ref, doT_ref, do_ref, oT_ref, lse_ref = refs[:9]
        dqT_ref, dk_ref, dv_ref, dq_s, delta_s = refs[-5:]
        g, qi, kj = pl.program_id(1), pl.program_id(2), pl.program_id(3)

        @pl.when(jnp.logical_and(jnp.logical_and(g == 0, qi == 0), kj == 0))
        def _():
            if acc_in:
                dk_ref[...] = refs[10][...]
                dv_ref[...] = refs[11][...]
            else:
                dk_ref[...] = jnp.zeros_like(dk_ref)
                dv_ref[...] = jnp.zeros_like(dv_ref)

        doT_v = doT_ref[...]

        @pl.when(kj == 0)
        def _():
            delta_s[...] = jnp.sum(doT_v.astype(F32) * oT_ref[...].astype(F32), axis=0, keepdims=True)
            dq_s[...] = jnp.zeros_like(dq_s)

        sT = jnp.dot(k_ref[...], qT_ref[...], preferred_element_type=F32)
        pT = jnp.exp2(sT * c2 - lse_ref[...] * LOG2E)
        rows = pl.ds(pl.multiple_of(kj * tk, tk), tk)
        dv_ref[rows, :] += jnp.dot(pT.astype(BF16), do_ref[...], preferred_element_type=F32)
        dpT = jnp.dot(v_ref[...], doT_v, preferred_element_type=F32)
        dsT = (pT * (dpT - delta_s[...]) * scale).astype(BF16)
        dk_ref[rows, :] += jnp.dot(dsT, q_ref[...], preferred_element_type=F32)
        dq_s[...] += jnp.dot(kT_ref[...], dsT, preferred_element_type=F32)

        @pl.when(kj == nk - 1)
        def _():
            dqT_ref[...] = dq_s[...].astype(dqT_ref.dtype)

    qh, kh = (lambda h, g: h * G + g), (lambda h, g: h)
    qb, kb = (lambda i, j: qoff + i), (lambda i, j: koff + j)
    kres = lambda i, j: k0 // k_len
    in_specs = [_cols_spec(hm, tq, d, qh, qb), _rows_spec(hm, tq, d, qh, qb), _rows_spec(hm, tk, d, kh, kb),
                _cols_spec(hm, tk, d, kh, kb), _rows_spec(False, tk, dv, kh, kb), _cols_spec(False, tq, dv, qh, qb),
                _rows_spec(False, tq, dv, qh, qb), _cols_spec(False, tq, dv, qh, qb),
                pl.BlockSpec((None, 1, tq), lambda h, g, i, j: (h * G + g, 0, qoff + i))]
    args, alias = [qT, q, k, kT, v, doT, do, oT, lse], {}
    dk_spec, dv_spec = _rows_spec(hm, k_len, d, kh, kres), _rows_spec(False, k_len, dv, kh, kres)
    if acc_in:
        in_specs += [HBM_SPEC, dk_spec, dv_spec]
        args += list(prev)
        alias = {9: 0, 10: 1, 11: 2}
    dq_shape = (H, d, total) if hm else (H * d, total)
    dk_shape = (Hk, total, d) if hm else (total, Hk * d)
    return pl.pallas_call(
        body, name=name, grid=(Hk, G, nq, nk), in_specs=in_specs,
        out_specs=(_cols_spec(hm, tq, d, qh, qb), dk_spec, dv_spec),
        out_shape=(jax.ShapeDtypeStruct(dq_shape, BF16), jax.ShapeDtypeStruct(dk_shape, F32),
                   jax.ShapeDtypeStruct((total, Hk * dv), F32)),
        scratch_shapes=[pltpu.VMEM((d, tq), F32), pltpu.VMEM((1, tq), F32)],
        input_output_aliases=alias,
        compiler_params=_params("parallel", "arbitrary", "arbitrary", "arbitrary"),
    )(*args)


def make_attention(*, hm, Hk, G, d, dv, scale, seq, total, name):
    kw = dict(hm=hm, Hk=Hk, G=G, d=d, dv=dv, scale=scale, total=total)
    lat = dict(q0=0, q_len=seq, k0=0, k_len=total)
    ctx = dict(q0=seq, q_len=total - seq, k0=seq, k_len=total - seq)
    tr = (lambda a: jnp.swapaxes(a, 1, 2)) if hm else (lambda a: a.T)

    def run(q, k, v):
        qT, vT = tr(q), v.T
        first = _attn_fwd_call(qT, k, vT, None, name=name + "_fwd_lat", **kw, **lat)
        return _attn_fwd_call(qT, k, vT, first, name=name + "_fwd_ctx", **kw, **ctx)

    @jax.custom_vjp
    def op(q, k, v):
        return run(q, k, v)[0]

    def fwd(q, k, v):
        oT, lse = run(q, k, v)
        return oT, (q, k, v, oT, lse)

    def bwd(res, doT):
        q, k, v, oT, lse = res
        ops = (tr(q), q, k, tr(k), v, doT, doT.T, oT, lse)
        first = _attn_bwd_call(*ops, None, name=name + "_bwd_lat", **kw, **lat)
        dqT, dk, dv_ = _attn_bwd_call(*ops, first, name=name + "_bwd_ctx", **kw, **ctx)
        return tr(dqT).astype(q.dtype), dk.astype(k.dtype), dv_.astype(v.dtype)

    op.defvjp(fwd, bwd)
    return op


def _loss_call(xf, target):
    S, D = target.shape
    tb = _pick(S, ROW_TILES)

    def body(y_ref, t_ref, dy_ref, acc_ref):
        @pl.when(pl.program_id(0) == 0)
        def _():
            acc_ref[...] = jnp.zeros_like(acc_ref)

        e = y_ref[...] - t_ref[...]
        dy_ref[...] = e * (1.0 / D)
        acc_ref[...] += jnp.sum(e * e, axis=0, keepdims=True)

    row = pl.BlockSpec((tb, D), lambda i: (i, 0))
    return pl.pallas_call(
        body, name="loss", grid=(S // tb,), in_specs=[row, row],
        out_specs=(row, pl.BlockSpec((1, D), lambda i: (0, 0))),
        out_shape=(jax.ShapeDtypeStruct((S, D), F32), jax.ShapeDtypeStruct((1, D), F32)),
        compiler_params=_params("arbitrary"),
    )(xf, target)


def _adamw_call(w, g, m, v, name):
    R, N = w.shape
    tb = _pick(R, tuple(t for t in (512, 256, 128, 64, 32, 16, 8) if t * N * 4 <= (1 << 20)) or (8,))
    bc1, bc2 = 1.0 - ADAM_B1 ** ADAM_STEP, 1.0 - ADAM_B2 ** ADAM_STEP

    def body(w_ref, g_ref, m_ref, v_ref, d_ref, nm_ref, nv_ref):
        gv = g_ref[...]
        nm = ADAM_B1 * m_ref[...] + (1.0 - ADAM_B1) * gv
        nv = ADAM_B2 * v_ref[...] + (1.0 - ADAM_B2) * (gv * gv)
        d_ref[...] = -ADAM_LR * ((nm / bc1) / (jnp.sqrt(nv / bc2) + ADAM_EPS) + ADAM_WD * w_ref[...])
        nm_ref[...] = nm
        nv_ref[...] = nv

    spec = pl.BlockSpec((tb, N), lambda i: (i, 0))
    shp = jax.ShapeDtypeStruct((R, N), F32)
    return pl.pallas_call(
        body, name=name, grid=(R // tb,), in_specs=[spec] * 4, out_specs=(spec,) * 3, out_shape=(shp,) * 3,
        compiler_params=_params("parallel"),
    )(w, g, m, v)


def _adamw(w, g, m, v, name="adamw"):
    shape = w.shape
    two = (lambda a: a.reshape(-1, shape[-1])) if w.ndim >= 2 else (lambda a: a.reshape(1, -1))
    return tuple(r.reshape(shape) for r in _adamw_call(two(w), two(g), two(m), two(v), name))


def _me():
    return lax.axis_index("x"), lax.axis_index("y"), lax.axis_index("c")


def _gather8(x, *, reduce, name):
    R, W = x.shape

    def body(x_ref, out_ref, *scratch):
        if reduce:
            buf, send_sems, recv_sems, local_sem = scratch
        else:
            buf = out_ref
            send_sems, recv_sems, local_sem = scratch
        x, y, c = _me()
        me, sibling = (x, y, c), (x, y, 1 - c)
        chips = [(1 - x, y), (x, 1 - y), (1 - x, 1 - y)]

        def rows(px, py, pc):
            return buf.at[pl.ds((4 * px + 2 * py + pc) * R, R), :]

        def copy(k, block, to, src=None):
            return pltpu.make_async_remote_copy(
                src_ref=rows(*block) if src is None else src, dst_ref=rows(*block),
                send_sem=send_sems.at[k], recv_sem=recv_sems.at[k], device_id=to, device_id_type=MESH)

        mine = pltpu.make_async_copy(x_ref, rows(*me), local_sem)
        mine.start()
        first = [copy(0, me, sibling, src=x_ref)]
        first += [copy(1 + j, me, (*chip, c), src=x_ref) for j, chip in enumerate(chips)]
        for cp in first:
            cp.start()
        passed = [copy(4 + j, (*chip, c), sibling) for j, chip in enumerate(chips)]
        for j, chip in enumerate(chips):
            copy(1 + j, (*chip, c), me).wait_recv()
            passed[j].start()
        copy(0, sibling, me).wait_recv()
        for j, chip in enumerate(chips):
            copy(4 + j, (*chip, 1 - c), me).wait_recv()
        for cp in first + passed:
            cp.wait_send()
        mine.wait()
        if reduce:
            acc = buf[0:R, :]
            for e in range(1, 8):
                acc = acc + buf[e * R:(e + 1) * R, :]
            out_ref[...] = acc

    sems = [pltpu.SemaphoreType.DMA((7,)), pltpu.SemaphoreType.DMA((7,)), pltpu.SemaphoreType.DMA]
    return pl.pallas_call(
        body, name=name,
        out_shape=jax.ShapeDtypeStruct((R if reduce else 8 * R, W), F32),
        in_specs=[pl.BlockSpec(memory_space=pltpu.VMEM)], out_specs=pl.BlockSpec(memory_space=pltpu.VMEM),
        scratch_shapes=([pltpu.VMEM((8 * R, W), F32)] if reduce else []) + sems,
        compiler_params=pltpu.CompilerParams(vmem_limit_bytes=VMEM_LIMIT),
    )(x)


def _small_exchange(flat, *, reduce, name):
    n = flat.shape[0]
    W = PACK_W if n >= 8 * PACK_W else LANES
    R = -(-n // (8 * W)) * 8
    x = jnp.pad(flat, (0, R * W - n)).reshape(R, W)
    out = _gather8(x, reduce=reduce, name=name)
    if reduce:
        return out.reshape(-1)[:n]
    return out.reshape(8, R * W)[:, :n]


HBM_SPEC = pl.BlockSpec(memory_space=pl.ANY)


def _chip_gather(mine):
    _, R, W = mine.shape

    def body(m_ref, out_ref, send_sems, recv_sems, local_sems):
        x, y, c = _me()
        sibling = (x, y, 1 - c)
        chips = [(1 - x, y), (x, 1 - y), (1 - x, 1 - y)]

        def slot(px, py, half):
            return out_ref.at[half, 2 * px + py]

        def copy(k, src, dst, to):
            return pltpu.make_async_remote_copy(src_ref=src, dst_ref=dst, send_sem=send_sems.at[k],
                                                recv_sem=recv_sems.at[k], device_id=to, device_id_type=MESH)

        local = [pltpu.make_async_copy(m_ref.at[hf], slot(x, y, hf), local_sems.at[hf]) for hf in range(2)]
        for cp in local:
            cp.start()
        first = [copy(j, m_ref.at[c], slot(x, y, c), (*chip, c)) for j, chip in enumerate(chips)]
        for cp in first:
            cp.start()
        passed = [copy(3 + j, slot(*chip, c), slot(*chip, c), sibling) for j, chip in enumerate(chips)]
        for j, chip in enumerate(chips):
            copy(j, m_ref.at[c], slot(*chip, c), (*chip, c)).wait_recv()
            passed[j].start()
        for j, chip in enumerate(chips):
            copy(3 + j, slot(*chip, 1 - c), slot(*chip, 1 - c), sibling).wait_recv()
        for cp in first + passed:
            cp.wait_send()
        for cp in local:
            cp.wait()

    return pl.pallas_call(
        body, name="chip_gather", out_shape=jax.ShapeDtypeStruct((2, 4, R, W), mine.dtype),
        in_specs=[HBM_SPEC], out_specs=HBM_SPEC,
        scratch_shapes=[pltpu.SemaphoreType.DMA((6,)), pltpu.SemaphoreType.DMA((6,)), pltpu.SemaphoreType.DMA((2,))],
    )(mine)


def _pair_swap(buf, *, pick_other_half, name):
    shape = buf.shape[1:] if pick_other_half else buf.shape

    def body(b_ref, out_ref, send_sem, recv_sem):
        x, y, c = _me()
        src = b_ref.at[1 - c] if pick_other_half else b_ref
        cp = pltpu.make_async_remote_copy(src_ref=src, dst_ref=out_ref, send_sem=send_sem, recv_sem=recv_sem,
                                          device_id=(x, y, 1 - c), device_id_type=MESH)
        cp.start()
        cp.wait()

    return pl.pallas_call(
        body, name=name, out_shape=jax.ShapeDtypeStruct(shape, buf.dtype), in_specs=[HBM_SPEC], out_specs=HBM_SPEC,
        scratch_shapes=[pltpu.SemaphoreType.DMA, pltpu.SemaphoreType.DMA],
    )(buf)


def _pair_share(half):
    def body(h_ref, out_ref, send_sem, recv_sem, local_sem):
        x, y, c = _me()
        local = pltpu.make_async_copy(h_ref, out_ref.at[c], local_sem)
        local.start()
        cp = pltpu.make_async_remote_copy(src_ref=h_ref, dst_ref=out_ref.at[c], send_sem=send_sem, recv_sem=recv_sem,
                                          device_id=(x, y, 1 - c), device_id_type=MESH)
        cp.start()
        pltpu.make_async_remote_copy(src_ref=h_ref, dst_ref=out_ref.at[1 - c], send_sem=send_sem, recv_sem=recv_sem,
                                     device_id=(x, y, 1 - c), device_id_type=MESH).wait_recv()
        cp.wait_send()
        local.wait()

    return pl.pallas_call(
        body, name="pair_share", out_shape=jax.ShapeDtypeStruct((2,) + half.shape, half.dtype), in_specs=[HBM_SPEC],
        out_specs=HBM_SPEC,
        scratch_shapes=[pltpu.SemaphoreType.DMA, pltpu.SemaphoreType.DMA, pltpu.SemaphoreType.DMA],
    )(half)


def _chip_exchange(s):
    def body(s_ref, out_ref, send_sems, recv_sems, local_sem):
        x, y, c = _me()
        q = 2 * x + y
        chips = [(1 - x, y), (x, 1 - y), (1 - x, 1 - y)]

        def copy(j, chip):
            return pltpu.make_async_remote_copy(
                src_ref=s_ref.at[2 * chip[0] + chip[1]], dst_ref=out_ref.at[q], send_sem=send_sems.at[j],
                recv_sem=recv_sems.at[j], device_id=(*chip, c), device_id_type=MESH)

        local = pltpu.make_async_copy(s_ref.at[q], out_ref.at[q], local_sem)
        local.start()
        sends = [copy(j, chip) for j, chip in enumerate(chips)]
        for cp in sends:
            cp.start()
        for j, chip in enumerate(chips):
            pltpu.make_async_remote_copy(
                src_ref=s_ref.at[q], dst_ref=out_ref.at[2 * chip[0] + chip[1]], send_sem=send_sems.at[j],
                recv_sem=recv_sems.at[j], device_id=(*chip, c), device_id_type=MESH).wait_recv()
        for cp in sends:
            cp.wait_send()
        local.wait()

    return pl.pallas_call(
        body, name="chip_exchange", out_shape=jax.ShapeDtypeStruct(s.shape, s.dtype), in_specs=[HBM_SPEC],
        out_specs=HBM_SPEC,
        scratch_shapes=[pltpu.SemaphoreType.DMA((3,)), pltpu.SemaphoreType.DMA((3,)), pltpu.SemaphoreType.DMA],
    )(s)


def _pair_sum(p, got, c):
    _, _, R, W = p.shape
    tb = _pick(R, ROW_TILES)

    def body(c_ref, p_ref, g_ref, o_ref):
        o_ref[...] = (p_ref[...].astype(F32) + g_ref[...].astype(F32)).astype(o_ref.dtype)

    return pl.pallas_call(
        body, name="pair_sum",
        grid_spec=pltpu.PrefetchScalarGridSpec(
            num_scalar_prefetch=1, grid=(4, R // tb),
            in_specs=[pl.BlockSpec((None, None, tb, W), lambda s, i, cr: (cr[0], s, i, 0)),
                      pl.BlockSpec((None, tb, W), lambda s, i, cr: (s, i, 0))],
            out_specs=pl.BlockSpec((None, tb, W), lambda s, i, cr: (s, i, 0))),
        out_shape=jax.ShapeDtypeStruct((4, R, W), BF16),
        compiler_params=_params("parallel", "parallel"),
    )(jnp.reshape(c, (1,)).astype(jnp.int32), p, got)


def _sum4(r):
    _, R, W = r.shape
    tb = _pick(R, ROW_TILES)

    def body(r_ref, o_ref):
        acc = r_ref[0].astype(F32)
        for a in range(1, 4):
            acc = acc + r_ref[a].astype(F32)
        o_ref[...] = acc

    return pl.pallas_call(
        body, name="sum4", grid=(R // tb,), in_specs=[pl.BlockSpec((4, tb, W), lambda i: (0, i, 0))],
        out_specs=pl.BlockSpec((tb, W), lambda i: (i, 0)), out_shape=jax.ShapeDtypeStruct((R, W), F32),
        compiler_params=_params("parallel"),
    )(r)


BIG = (("mla_w_dq", 1), ("mla_w_uq", 2), ("mla_w_dkv", 1), ("mla_w_ukv", 2), ("mla_w_o", 1),
       ("gqa_w_q", 1), ("gqa_w_kv", 1), ("gqa_w_o", 1), ("ffn_w_up", 2), ("ffn_w_down", 1))


def _pack_rows(n):
    return -(-n // (2 * PACK_W * ROW_TILES[0])) * ROW_TILES[0]


def _to_shards(full, axis):
    L, K, N = full.shape
    if axis == 1:
        return full.reshape(L, 4, K // 4, N).transpose(1, 0, 2, 3).reshape(4, -1)
    return full.reshape(L, K, 4, N // 4).transpose(2, 0, 1, 3).reshape(4, -1)


def _from_shards(flat4, shard_shape, axis):
    L, K, N = shard_shape
    a = flat4.reshape(4, L, K, N)
    if axis == 1:
        return a.transpose(1, 0, 2, 3).reshape(L, 4 * K, N)
    return a.transpose(1, 2, 0, 3).reshape(L, K, 4 * N)


def _rope_tables(seq, total, rot_dim):
    t = jnp.arange(seq, dtype=jnp.int32)
    rows, cols = t // GRID_W, t % GRID_W
    axis_dim = rot_dim // 2
    inv = jnp.power(ROPE_BASE, -jnp.arange(0, axis_dim, 2, dtype=F32) / axis_dim)
    ang_r = rows.astype(F32)[:, None] * inv
    ang_c = cols.astype(F32)[:, None] * inv
    ang = jnp.concatenate([ang_r, ang_r, ang_c, ang_c], axis=-1)
    cos = jnp.concatenate([jnp.cos(ang), jnp.ones((total - seq, rot_dim), F32)], axis=0)
    sin = jnp.concatenate([jnp.sin(ang), jnp.zeros((total - seq, rot_dim), F32)], axis=0)
    rep = LANES // rot_dim
    return jnp.tile(cos, (1, rep)), jnp.tile(sin, (1, rep))


def _forward(X, mod, P, *, seq):
    T, D = X.shape
    depth = mod.shape[0]
    mask = (jnp.arange(T) >= seq).astype(F32)[:, None]
    rope_m = _rope_tables(seq, T, ROPE_D)
    rope_g = _rope_tables(seq, T, HEAD)
    Hm = P["mla_w_uq"].shape[-1] // (NOPE + ROPE_D)
    Hg = P["gqa_w_q"].shape[-1] // HEAD
    Hkv = P["gqa_w_kv"].shape[-1] // (2 * HEAD)
    kv_rank = P["mla_w_dkv"].shape[-1] - ROPE_D
    dff = P["ffn_conv_b"].shape[-1]

    norm_row = make_norm(P["mla_w_dq"].shape[-1], name="norm_rank")
    norm_kv = make_norm(kv_rank, name="norm_kvrank")
    norm_head = make_norm(LANES, name="norm_head")
    norm_pe = make_norm(ROPE_D, ROPE_D // 4, name="norm_pe")
    norm_gqa = make_norm(HEAD, HEAD // 4, name="norm_gqa")
    attn_mla = make_attention(hm=True, Hk=Hm, G=1, d=NOPE + ROPE_D, dv=VDIM, scale=1.0 / math.sqrt(NOPE + ROPE_D),
                              seq=seq, total=T, name="mla")
    attn_gqa = make_attention(hm=False, Hk=Hkv, G=Hg // Hkv, d=HEAD, dv=HEAD, scale=1.0 / math.sqrt(HEAD),
                              seq=seq, total=T, name="gqa")
    convgate = make_convgate(seq)
    row = lambda g: g.reshape(1, -1)
    twice = lambda g: jnp.concatenate([g, g]).reshape(1, -1)

    for i in range(depth):
        j = i // 2
        sh1, sc1, g1, sh2, sc2, g2 = [mod[i][:, k * D:(k + 1) * D] for k in range(N_MOD)]
        h = modnorm(X, row(P["norm_mix"][i]), sh1, sc1, mask)
        if i % 2 == 0:
            w_uq = P["mla_w_uq"][j].reshape(-1, Hm, NOPE + ROPE_D)
            w_ukv = P["mla_w_ukv"][j].reshape(-1, Hm, NOPE + VDIM)
            cq = norm_row(linear(h, P["mla_w_dq"][j]), row(P["mla_g_dq"][j]), None, None)
            qn = norm_head(linear(cq, w_uq[:, :, :NOPE].reshape(-1, Hm * NOPE)), row(P["mla_g_q_nope"][j]), None, None)
            qp = norm_pe(linear(cq, w_uq[:, :, NOPE:].reshape(-1, Hm * ROPE_D)), twice(P["mla_g_q_pe"][j]), *rope_m)
            kva = linear(h, P["mla_w_dkv"][j])
            ckv = norm_kv(kva[:, :kv_rank], row(P["mla_g_dkv"][j]), None, None)
            kp_raw = kva[:, kv_rank:]
            kp = norm_pe(jnp.concatenate([kp_raw, kp_raw], axis=-1), twice(P["mla_g_k_pe"][j]), *rope_m)[:, :ROPE_D]
            kn = norm_head(linear(ckv, w_ukv[:, :, :NOPE].reshape(-1, Hm * NOPE)), row(P["mla_g_k_nope"][j]), None, None)
            v = linear_bf16(ckv, w_ukv[:, :, NOPE:].reshape(-1, Hm * VDIM))
            q_hm = jnp.concatenate([qn.reshape(T, Hm, NOPE), qp.reshape(T, Hm, ROPE_D)], axis=-1).transpose(1, 0, 2)
            k_hm = jnp.concatenate([kn.reshape(T, Hm, NOPE), jnp.broadcast_to(kp[:, None, :], (T, Hm, ROPE_D))],
                                   axis=-1).transpose(1, 0, 2)
            oT = attn_mla(q_hm, k_hm, v)
            X = linear_gated_t(oT, P["mla_w_o"][j], X, g1, mask)
        else:
            w_kv = P["gqa_w_kv"][j]
            q = norm_gqa(linear(h, P["gqa_w_q"][j]), row(P["gqa_g_q"][j]), *rope_g)
            k = norm_gqa(linear(h, w_kv[:, :Hkv * HEAD]), row(P["gqa_g_k"][j]), *rope_g)
            v = linear_bf16(h, w_kv[:, Hkv * HEAD:])
            oT = attn_gqa(q, k, v)
            X = linear_gated_t(oT, P["gqa_w_o"][j], X, g1, mask)
        h2 = modnorm(X, row(P["norm_ffn"][i]), sh2, sc2, mask)
        w_up = P["ffn_w_up"][i]
        ug = linear_bf16(h2, w_up[:, :dff])
        uv = linear_bf16(h2, w_up[:, dff:])
        a = convgate(ug, uv, P["ffn_conv_w"][i], row(P["ffn_conv_b"][i]))
        X = linear_gated(a, P["ffn_w_down"][i], X, g2, mask)
    return X


SMALL = ("norm_mix", "norm_ffn", "mla_g_dq", "mla_g_q_nope", "mla_g_q_pe", "mla_g_dkv", "mla_g_k_pe",
         "mla_g_k_nope", "gqa_g_q", "gqa_g_k", "ffn_conv_w", "ffn_conv_b")
WEIGHTS = ("c_ctx", "w_mod", "b_mod", "norm_mix", "norm_ffn", "mla_w_dq", "mla_g_dq", "mla_w_uq", "mla_g_q_nope",
           "mla_g_q_pe", "mla_w_dkv", "mla_g_dkv", "mla_g_k_pe", "mla_w_ukv", "mla_g_k_nope", "mla_w_o", "gqa_w_q",
           "gqa_g_q", "gqa_w_kv", "gqa_g_k", "gqa_w_o", "ffn_w_up", "ffn_conv_w", "ffn_conv_b", "ffn_w_down")


def _step(A):
    x, y, c = _me()
    q = 2 * x + y
    dev = 4 * x + 2 * y + c
    seq, D = A["x"].shape[1], A["x"].shape[2]
    depth = A["w_mod"].shape[0]
    X0 = jnp.concatenate([A["x"][0], A["ctx"][0]], axis=0)

    shard_n = [math.prod(A[n].shape) for n, _ in BIG]
    n_tot = sum(shard_n)
    R = _pack_rows(n_tot)
    flat = jnp.concatenate([A[n].astype(BF16).reshape(-1) for n, _ in BIG])
    mine = jnp.pad(flat, (0, 2 * R * PACK_W - n_tot)).reshape(2, R, PACK_W)
    allw = _chip_gather(mine).transpose(1, 0, 2, 3).reshape(4, -1)
    P, off = {}, 0
    for (n, axis), cnt in zip(BIG, shard_n):
        P[n] = _from_shards(allw[:, off:off + cnt], A[n].shape, axis)
        off += cnt

    c_all = _small_exchange(A["c"].reshape(-1), reduce=False, name="gather_c")
    cw = _small_exchange(A["ffn_conv_w"].reshape(-1), reduce=False, name="gather_convw")
    cw = cw.reshape(4, 2, depth, 3, -1)[:, 0].transpose(1, 2, 0, 3).reshape(depth, 3, -1)
    cvec = jnp.concatenate([c_all, jnp.broadcast_to(A["c_ctx"][None, :], (8, D))], axis=0)
    act = jax.nn.silu(cvec)
    ncol = A["w_mod"].shape[-1]
    modcols = jnp.stack([_mm(act, A["w_mod"][i], name="mod_fwd") for i in range(depth)])
    mod_all = _small_exchange(modcols.reshape(-1), reduce=False, name="gather_mod")
    mod_all = mod_all.reshape(4, 2, depth, 16, ncol)[:, 0].transpose(1, 2, 0, 3).reshape(depth, 16, 4 * ncol)
    mod_all = mod_all + A["b_mod"][:, None, :]
    mod = jnp.stack([lax.dynamic_index_in_dim(mod_all, dev, 1, keepdims=False), mod_all[:, 8]], axis=1)

    for n in SMALL:
        P[n] = cw if n == "ffn_conv_w" else A[n]

    XF, vjp = jax.vjp(functools.partial(_forward, seq=seq), X0, mod, P)
    dy, sq = _loss_call(XF, A["loss_target"][0])
    loss = lax.psum(0.5 / D * jnp.sum(sq), ("x", "y", "c"))
    dX0, dmod, dP = vjp(jnp.concatenate([dy, jnp.zeros((X0.shape[0] - seq, D), F32)], axis=0))
    grad_x = dX0[:seq][None]

    G = {}
    dmod_all = _small_exchange(dmod.reshape(-1), reduce=False, name="gather_dmod").reshape(8, depth, 2, -1)
    drows = jnp.concatenate([dmod_all[:, :, 0], dmod_all[:, :, 1]], axis=0).transpose(1, 0, 2)
    G["b_mod"] = jnp.sum(drows, axis=1)
    dcols = lax.dynamic_slice_in_dim(drows, q * ncol, ncol, axis=2)
    G["w_mod"] = jnp.stack([_mm(act, dcols[i], ta=True, name="mod_dw") for i in range(depth)])
    dact = sum(_mm(dcols[i], A["w_mod"][i], tb=True, name="mod_dact") for i in range(depth))
    dcc_part = 0.5 * jnp.sum(dact[8:], axis=0)

    small_list = [dP[n].astype(F32).reshape(-1) for n in SMALL] + [dcc_part]
    sizes = [a.shape[0] for a in small_list]
    tot = _small_exchange(jnp.concatenate(small_list), reduce=True, name="reduce_small")
    off = 0
    for n, cnt in zip(SMALL + ("c_ctx",), sizes):
        G[n] = tot[off:off + cnt].reshape(P[n].shape if n != "c_ctx" else (D,))
        off += cnt
    sg = jax.nn.sigmoid(A["c_ctx"])
    G["c_ctx"] = G["c_ctx"] * (sg * (1.0 + A["c_ctx"] * (1.0 - sg)))
    fcols = A["ffn_conv_w"].shape[-1]
    G["ffn_conv_w"] = lax.dynamic_slice_in_dim(G["ffn_conv_w"], q * fcols, fcols, axis=2)

    gflat = jnp.concatenate([_to_shards(dP[n], axis) for n, axis in BIG], axis=1)
    gp = jnp.pad(gflat, ((0, 0), (0, 2 * R * PACK_W - n_tot))).reshape(4, 2, R, PACK_W).transpose(1, 0, 2, 3)
    got = _pair_swap(gp, pick_other_half=True, name="pair_swap_grads")
    part = _chip_exchange(_pair_sum(gp, got, c))
    half = _sum4(part)
    both = _pair_share(half).reshape(-1)
    off = 0
    for (n, _), cnt in zip(BIG, shard_n):
        G[n] = both[off:off + cnt].reshape(A[n].shape)
        off += cnt

    delta, new_m, new_v = {}, {}, {}
    tiny = [n for n in WEIGHTS if A[n].size < (1 << 18)]
    cat = lambda pre: jnp.concatenate([(A[pre + n] if pre != "g" else G[n]).reshape(-1) for n in tiny])
    n_tiny = sum(A[n].size for n in tiny)
    rt = -(-n_tiny // (8 * PACK_W)) * 8
    shape2 = lambda a: jnp.pad(a, (0, rt * PACK_W - n_tiny)).reshape(rt, PACK_W)
    outs = _adamw_call(shape2(cat("")), shape2(cat("g")), shape2(cat("m_")), shape2(cat("v_")), "adamw_small")
    off = 0
    for n in tiny:
        cnt = A[n].size
        delta[n], new_m[n], new_v[n] = [o.reshape(-1)[off:off + cnt].reshape(A[n].shape) for o in outs]
        off += cnt
    for n in WEIGHTS:
        if n not in tiny:
            delta[n], new_m[n], new_v[n] = _adamw(A[n], G[n], A["m_" + n], A["v_" + n])
    return (loss, grad_x, *[G[n] for n in WEIGHTS], *[delta[n] for n in WEIGHTS],
            *[new_m[n] for n in WEIGHTS], *[new_v[n] for n in WEIGHTS])


def kernel(x, c, ctx, c_ctx, w_mod, b_mod, norm_mix, norm_ffn, mla_w_dq, mla_g_dq, mla_w_uq, mla_g_q_nope, mla_g_q_pe, mla_w_dkv, mla_g_dkv, mla_g_k_pe, mla_w_ukv, mla_g_k_nope, mla_w_o, gqa_w_q, gqa_g_q, gqa_w_kv, gqa_g_k, gqa_w_o, ffn_w_up, ffn_conv_w, ffn_conv_b, ffn_w_down, loss_target, m_c_ctx, m_w_mod, m_b_mod, m_norm_mix, m_norm_ffn, m_mla_w_dq, m_mla_g_dq, m_mla_w_uq, m_mla_g_q_nope, m_mla_g_q_pe, m_mla_w_dkv, m_mla_g_dkv, m_mla_g_k_pe, m_mla_w_ukv, m_mla_g_k_nope, m_mla_w_o, m_gqa_w_q, m_gqa_g_q, m_gqa_w_kv, m_gqa_g_k, m_gqa_w_o, m_ffn_w_up, m_ffn_conv_w, m_ffn_conv_b, m_ffn_w_down, v_c_ctx, v_w_mod, v_b_mod, v_norm_mix, v_norm_ffn, v_mla_w_dq, v_mla_g_dq, v_mla_w_uq, v_mla_g_q_nope, v_mla_g_q_pe, v_mla_w_dkv, v_mla_g_dkv, v_mla_g_k_pe, v_mla_w_ukv, v_mla_g_k_nope, v_mla_w_o, v_gqa_w_q, v_gqa_g_q, v_gqa_w_kv, v_gqa_g_k, v_gqa_w_o, v_ffn_w_up, v_ffn_conv_w, v_ffn_conv_b, v_ffn_w_down):
    return _step(dict(locals()))
```

```python
import functools
import math

import jax
import jax.numpy as jnp
from jax import lax
from jax.experimental import pallas as pl
from jax.experimental.pallas import tpu as pltpu

F32, BF16 = jnp.float32, jnp.bfloat16
MESH = pl.DeviceIdType.MESH

EPS = 1e-6
ROPE_BASE = 10000.0
GRID_W = 64
NOPE, ROPE_D, VDIM = 128, 64, 128
HEAD = 128
N_MOD = 6
LANES = 128
HALO = 16
VMEM_LIMIT = 56 * 1024 * 1024
PACK_W = 1024

ADAM_LR, ADAM_B1, ADAM_B2, ADAM_EPS, ADAM_WD, ADAM_STEP = 0.001, 0.9, 0.999, 1e-08, 0.01, 10

LANE_TILES = (1024, 768, 512, 256, 128)
K_RESIDENT = 2048
ROW_TILES = (256, 128, 64, 32, 16, 8)


def _pick(n, prefs):
    for p in prefs:
        if n % p == 0:
            return p
    return n


def _params(*sem):
    return pltpu.CompilerParams(dimension_semantics=sem, vmem_limit_bytes=VMEM_LIMIT)


def _mm(a, b, *, ta=False, tb=False, out_dtype=F32, resid=None, gate=None, mask=None, name="mm"):
    (K, M) = a.shape if ta else a.shape[::-1]
    N = b.shape[0] if tb else b.shape[1]
    assert (b.shape[1] if tb else b.shape[0]) == K
    tm, tn = _pick(M, LANE_TILES), _pick(N, LANE_TILES)
    tk = K if K <= K_RESIDENT else _pick(K, (1408,) + LANE_TILES)
    nk = K // tk
    fused = resid is not None
    dn = (((0 if ta else 1,), (1 if tb else 0,)), ((), ()))

    def body(*refs):
        refs = list(refs)
        acc_ref = refs.pop() if nk > 1 else None
        if fused:
            a_ref, b_ref, r_ref, g_ref, m_ref, o_ref, y_ref = refs
        else:
            a_ref, b_ref, o_ref = refs
        part = lax.dot_general(a_ref[...].astype(BF16), b_ref[...].astype(BF16), dn, preferred_element_type=F32)

        def finish(acc):
            if fused:
                g = jnp.where(m_ref[...] > 0.5, g_ref[1:2, :], g_ref[0:1, :])
                y_ref[...] = acc.astype(y_ref.dtype)
                o_ref[...] = r_ref[...] + g * acc
            else:
                o_ref[...] = acc.astype(o_ref.dtype)

        if nk == 1:
            finish(part)
        else:
            k = pl.program_id(2)

            @pl.when(k == 0)
            def _():
                acc_ref[...] = part

            @pl.when(k > 0)
            def _():
                acc_ref[...] += part

            @pl.when(k == nk - 1)
            def _():
                finish(acc_ref[...])

    a_spec = pl.BlockSpec((tk, tm), lambda i, j, k: (k, i)) if ta else pl.BlockSpec((tm, tk), lambda i, j, k: (i, k))
    b_spec = pl.BlockSpec((tn, tk), lambda i, j, k: (j, k)) if tb else pl.BlockSpec((tk, tn), lambda i, j, k: (k, j))
    o_spec = pl.BlockSpec((tm, tn), lambda i, j, k: (i, j))
    in_specs, args = [a_spec, b_spec], [a, b]
    out_shape, out_specs = jax.ShapeDtypeStruct((M, N), out_dtype), o_spec
    if fused:
        in_specs += [o_spec, pl.BlockSpec((2, tn), lambda i, j, k: (0, j)), pl.BlockSpec((tm, 1), lambda i, j, k: (i, 0))]
        args += [resid, gate, mask]
        out_shape = (jax.ShapeDtypeStruct((M, N), F32), jax.ShapeDtypeStruct((M, N), BF16))
        out_specs = (o_spec, o_spec)
    return pl.pallas_call(
        body, name=name, grid=(M // tm, N // tn, nk), in_specs=in_specs, out_specs=out_specs, out_shape=out_shape,
        scratch_shapes=[pltpu.VMEM((tm, tn), F32)] if nk > 1 else [],
        compiler_params=_params("parallel", "parallel", "arbitrary"),
    )(*args)


@jax.custom_vjp
def linear(a, w):
    return _mm(a, w, out_dtype=F32, name="lin_fwd")


def _linear_fwd(a, w):
    return linear(a, w), (a, w)


def _linear_bwd(res, dy):
    a, w = res
    return (_mm(dy, w, tb=True, out_dtype=a.dtype, name="lin_da"),
            _mm(a, dy, ta=True, out_dtype=w.dtype, name="lin_dw"))


linear.defvjp(_linear_fwd, _linear_bwd)


@jax.custom_vjp
def linear_bf16(a, w):
    return _mm(a, w, out_dtype=BF16, name="linb_fwd")


def _linear_bf16_fwd(a, w):
    return linear_bf16(a, w), (a, w)


linear_bf16.defvjp(_linear_bf16_fwd, _linear_bwd)


def _gate_bwd(dx, y, gate, mask):
    T, N = dx.shape
    tb = _pick(T, ROW_TILES)

    def body(dx_ref, y_ref, g_ref, m_ref, dy_ref, dg_ref):
        @pl.when(pl.program_id(0) == 0)
        def _():
            dg_ref[...] = jnp.zeros_like(dg_ref)

        d = dx_ref[...]
        ctx = m_ref[...] > 0.5
        dy_ref[...] = (jnp.where(ctx, g_ref[1:2, :], g_ref[0:1, :]) * d).astype(dy_ref.dtype)
        dyy = d * y_ref[...].astype(F32)
        dg_ref[0:1, :] += jnp.sum(jnp.where(ctx, 0.0, dyy), axis=0, keepdims=True)
        dg_ref[1:2, :] += jnp.sum(jnp.where(ctx, dyy, 0.0), axis=0, keepdims=True)

    row = pl.BlockSpec((tb, N), lambda i: (i, 0))
    small = pl.BlockSpec((2, N), lambda i: (0, 0))
    return pl.pallas_call(
        body, name="gate_bwd", grid=(T // tb,),
        in_specs=[row, row, small, pl.BlockSpec((tb, 1), lambda i: (i, 0))], out_specs=(row, small),
        out_shape=(jax.ShapeDtypeStruct((T, N), BF16), jax.ShapeDtypeStruct((2, N), F32)),
        compiler_params=_params("arbitrary"),
    )(dx, y, gate, mask)


def _make_linear_gated(at):
    @jax.custom_vjp
    def op(a, w, x, gate, mask):
        return _mm(a, w, ta=at, resid=x, gate=gate, mask=mask, name="ling_fwd")[0]

    def fwd(a, w, x, gate, mask):
        out, y = _mm(a, w, ta=at, resid=x, gate=gate, mask=mask, name="ling_fwd")
        return out, (a, w, y, gate, mask)

    def bwd(res, dx):
        a, w, y, gate, mask = res
        dy, dgate = _gate_bwd(dx, y, gate, mask)
        if at:
            da = _mm(w, dy, tb=True, out_dtype=a.dtype, name="ling_da_t")
            dw = _mm(a, dy, out_dtype=w.dtype, name="ling_dw_t")
        else:
            da = _mm(dy, w, tb=True, out_dtype=a.dtype, name="ling_da")
            dw = _mm(a, dy, ta=True, out_dtype=w.dtype, name="ling_dw")
        return da, dw, dx, dgate, None

    op.defvjp(fwd, bwd)
    return op


linear_gated = _make_linear_gated(False)
linear_gated_t = _make_linear_gated(True)


def _rope_rot(x, quarter):
    lane = lax.broadcasted_iota(jnp.int32, x.shape, 1)
    lo = (lane % (2 * quarter)) < quarter
    return jnp.where(lo, -pltpu.roll(x, LANES - quarter, 1), pltpu.roll(x, quarter, 1))


def _group_mean(v, group):
    if group == LANES:
        return jnp.mean(v, axis=-1, keepdims=True)
    lane = lax.broadcasted_iota(jnp.int32, v.shape, 1)
    lo = lane < group
    s_lo = jnp.sum(jnp.where(lo, v, 0.0), axis=-1, keepdims=True)
    s_hi = jnp.sum(jnp.where(lo, 0.0, v), axis=-1, keepdims=True)
    return jnp.where(lo, s_lo, s_hi) * (1.0 / group)


def _norm_call(x, gain, dy, *, group, quarter, cos, sin, shift, scale, mask, out_dtype, name):
    T, W = x.shape
    whole = group == W
    use_rope, use_mod, bwd = cos is not None, shift is not None, dy is not None
    assert not (whole and use_rope) and not (use_mod and not whole)
    tb = _pick(T, ROW_TILES)
    nchunk = W // LANES

    def body(*refs):
        refs = list(refs)
        x_ref, g_ref = refs.pop(0), refs.pop(0)
        dy_ref = refs.pop(0) if bwd else None
        cos_ref, sin_ref = (refs.pop(0), refs.pop(0)) if use_rope else (None, None)
        sh_ref, sc_ref, m_ref = (refs.pop(0), refs.pop(0), refs.pop(0)) if use_mod else (None, None, None)
        if not bwd:
            (o_ref,) = refs
        elif use_mod:
            dx_ref, dg_ref, dsh_ref, dsc_ref = refs
        else:
            dx_ref, dg_ref = refs

        if bwd:
            @pl.when(pl.program_id(0) == 0)
            def _():
                dg_ref[...] = jnp.zeros_like(dg_ref)
                if use_mod:
                    dsh_ref[...] = jnp.zeros_like(dsh_ref)
                    dsc_ref[...] = jnp.zeros_like(dsc_ref)

        if whole:
            xv = x_ref[...].astype(F32)
            r = lax.rsqrt(jnp.mean(xv * xv, axis=-1, keepdims=True) + EPS)
            xn = xv * r
            gain_v = g_ref[...]
            if use_mod:
                ctx = m_ref[...] > 0.5
                sc = jnp.where(ctx, sc_ref[1:2, :], sc_ref[0:1, :])
            if not bwd:
                y = xn * gain_v
                if use_mod:
                    y = y * (1.0 + sc) + jnp.where(ctx, sh_ref[1:2, :], sh_ref[0:1, :])
                o_ref[...] = y.astype(o_ref.dtype)
            else:
                d = dy_ref[...].astype(F32)
                if use_mod:
                    dyy = d * (xn * gain_v)
                    dsh_ref[0:1, :] += jnp.sum(jnp.where(ctx, 0.0, d), axis=0, keepdims=True)
                    dsh_ref[1:2, :] += jnp.sum(jnp.where(ctx, d, 0.0), axis=0, keepdims=True)
                    dsc_ref[0:1, :] += jnp.sum(jnp.where(ctx, 0.0, dyy), axis=0, keepdims=True)
                    dsc_ref[1:2, :] += jnp.sum(jnp.where(ctx, dyy, 0.0), axis=0, keepdims=True)
                    d = d * (1.0 + sc)
                dg_ref[...] += jnp.sum(d * xn, axis=0, keepdims=True)
                dxn = d * gain_v
                dx_ref[...] = (r * (dxn - xn * jnp.mean(dxn * xn, axis=-1, keepdims=True))).astype(dx_ref.dtype)
        else:
            gain_v = g_ref[...]
            if use_rope:
                cs, sn = cos_ref[...], sin_ref[...]
            for c in range(nchunk):
                cols = slice(c * LANES, (c + 1) * LANES)
                xv = x_ref[:, cols].astype(F32)
                r = lax.rsqrt(_group_mean(xv * xv, group) + EPS)
                xn = xv * r
                if not bwd:
                    y = xn * gain_v
                    if use_rope:
                        y = y * cs + _rope_rot(y, quarter) * sn
                    o_ref[:, cols] = y.astype(o_ref.dtype)
                else:
                    d = dy_ref[:, cols].astype(F32)
                    if use_rope:
                        d = d * cs - _rope_rot(d * sn, quarter)
                    dg_ref[...] += jnp.sum(d * xn, axis=0, keepdims=True)
                    dxn = d * gain_v
                    dx_ref[:, cols] = (r * (dxn - xn * _group_mean(dxn * xn, group))).astype(dx_ref.dtype)

    row = pl.BlockSpec((tb, W), lambda i: (i, 0))
    gw = W if whole else LANES
    gspec = pl.BlockSpec((1, gw), lambda i: (0, 0))
    in_specs, args = [row, gspec], [x, gain]
    if bwd:
        in_specs.append(row)
        args.append(dy)
    if use_rope:
        tab = pl.BlockSpec((tb, LANES), lambda i: (i, 0))
        in_specs += [tab, tab]
        args += [cos, sin]
    if use_mod:
        two = pl.BlockSpec((2, W), lambda i: (0, 0))
        in_specs += [two, two, pl.BlockSpec((tb, 1), lambda i: (i, 0))]
        args += [shift, scale, mask]
    if not bwd:
        out_shape, out_specs = jax.ShapeDtypeStruct((T, W), out_dtype), row
    else:
        out_shape = [jax.ShapeDtypeStruct((T, W), x.dtype), jax.ShapeDtypeStruct((1, gw), F32)]
        out_specs = [row, gspec]
        if use_mod:
            out_shape += [jax.ShapeDtypeStruct((2, W), F32)] * 2
            out_specs += [pl.BlockSpec((2, W), lambda i: (0, 0))] * 2
    return pl.pallas_call(
        body, name=name, grid=(T // tb,), in_specs=in_specs, out_specs=out_specs, out_shape=out_shape,
        compiler_params=_params("arbitrary" if bwd else "parallel"),
    )(*args)


def make_norm(group, quarter=0, name="norm"):
    kw = dict(group=group, quarter=quarter, shift=None, scale=None, mask=None, out_dtype=BF16)

    @jax.custom_vjp
    def op(x, gain, cos, sin):
        return _norm_call(x, gain, None, cos=cos, sin=sin, name=name + "_fwd", **kw)

    def fwd(x, gain, cos, sin):
        return op(x, gain, cos, sin), (x, gain, cos, sin)

    def bwd(res, dy):
        x, gain, cos, sin = res
        dx, dg = _norm_call(x, gain, dy, cos=cos, sin=sin, name=name + "_bwd", **kw)
        return dx, dg, None, None

    op.defvjp(fwd, bwd)
    return op


@jax.custom_vjp
def modnorm(x, gain, shift, scale, mask):
    return _norm_call(x, gain, None, group=x.shape[1], quarter=0, cos=None, sin=None, shift=shift, scale=scale,
                      mask=mask, out_dtype=BF16, name="modnorm_fwd")


def _modnorm_fwd(x, gain, shift, scale, mask):
    return modnorm(x, gain, shift, scale, mask), (x, gain, shift, scale, mask)


def _modnorm_bwd(res, dy):
    x, gain, shift, scale, mask = res
    dx, dg, dsh, dsc = _norm_call(x, gain, dy, group=x.shape[1], quarter=0, cos=None, sin=None, shift=shift,
                                  scale=scale, mask=mask, out_dtype=BF16, name="modnorm_bwd")
    return dx, dg, dsh, dsc, None


modnorm.defvjp(_modnorm_fwd, _modnorm_bwd)


def _conv_call(ug, uv, cw, cb, da, *, seq, name):
    T, F = ug.shape
    bwd = da is not None
    tb = _pick(math.gcd(seq, T - seq), ROW_TILES)
    tf = _pick(F, (512, 256, 128))
    nlat, ntot, hpb = seq // tb, T // tb, tb // HALO
    n_ext = tb + 2 * HALO

    def body(*refs):
        if bwd:
            (g_ref, gp_ref, gn_ref, v_ref, vp_ref, vn_ref, d_ref, dp_ref, dn_ref, w_ref, b_ref,
             dg_ref, dv_ref, dw_ref, db_ref) = refs
        else:
            g_ref, gp_ref, gn_ref, v_ref, w_ref, b_ref, o_ref = refs
        i = pl.program_id(1)
        first = jnp.logical_or(i == 0, i == nlat)
        last = jnp.logical_or(i == nlat - 1, i == ntot - 1)
        w = w_ref[...]
        w0, w1, w2, b = w[0:1, :], w[1:2, :], w[2:3, :], b_ref[...]
        if not bwd:
            g = g_ref[...].astype(F32)
            hrow = lax.broadcasted_iota(jnp.int32, (HALO, tf), 0)
            prev = jnp.sum(jnp.where(hrow == HALO - 1, gp_ref[...].astype(F32), 0.0), axis=0, keepdims=True)
            nxt = jnp.sum(jnp.where(hrow == 0, gn_ref[...].astype(F32), 0.0), axis=0, keepdims=True)
            prev = jnp.where(first, 0.0, prev)
            nxt = jnp.where(last, 0.0, nxt)
            row = lax.broadcasted_iota(jnp.int32, g.shape, 0)
            gm1 = jnp.where(row == 0, prev, pltpu.roll(g, 1, 0))
            gp1 = jnp.where(row == tb - 1, nxt, pltpu.roll(g, tb - 1, 0))
            pre = w0 * gm1 + w1 * g + w2 * gp1 + b
            o_ref[...] = (pre * jax.nn.sigmoid(pre) * v_ref[...].astype(F32)).astype(o_ref.dtype)
        else:
            @pl.when(i == 0)
            def _():
                dw_ref[...] = jnp.zeros_like(dw_ref)
                db_ref[...] = jnp.zeros_like(db_ref)

            def ext(p_ref, m_ref, n_ref):
                return jnp.concatenate([p_ref[...].astype(F32), m_ref[...].astype(F32), n_ref[...].astype(F32)], axis=0)

            row = lax.broadcasted_iota(jnp.int32, (n_ext, tf), 0)
            valid = jnp.logical_and(jnp.logical_or(row >= HALO, jnp.logical_not(first)),
                                    jnp.logical_or(row < HALO + tb, jnp.logical_not(last)))
            ge = jnp.where(valid, ext(gp_ref, g_ref, gn_ref), 0.0)
            ve = ext(vp_ref, v_ref, vn_ref)
            de = ext(dp_ref, d_ref, dn_ref)
            gm1 = pltpu.roll(ge, 1, 0)
            gp1 = pltpu.roll(ge, n_ext - 1, 0)
            pre = w0 * gm1 + w1 * ge + w2 * gp1 + b
            sg = jax.nn.sigmoid(pre)
            dpre = jnp.where(valid, de * ve * (sg * (1.0 + pre * (1.0 - sg))), 0.0)
            dge = w0 * pltpu.roll(dpre, n_ext - 1, 0) + w1 * dpre + w2 * pltpu.roll(dpre, 1, 0)
            mid = slice(HALO, HALO + tb)
            dg_ref[...] = dge[mid].astype(dg_ref.dtype)
            dv_ref[...] = (de * pre * sg)[mid].astype(dv_ref.dtype)
            dw_ref[0:1, :] += jnp.sum((dpre * gm1)[mid], axis=0, keepdims=True)
            dw_ref[1:2, :] += jnp.sum((dpre * ge)[mid], axis=0, keepdims=True)
            dw_ref[2:3, :] += jnp.sum((dpre * gp1)[mid], axis=0, keepdims=True)
            db_ref[...] += jnp.sum(dpre[mid], axis=0, keepdims=True)

    main = pl.BlockSpec((tb, tf), lambda j, i: (i, j))
    prev = pl.BlockSpec((HALO, tf), lambda j, i: (jnp.maximum(i * hpb - 1, 0), j))
    nxt = pl.BlockSpec((HALO, tf), lambda j, i: (jnp.minimum((i + 1) * hpb, T // HALO - 1), j))
    wspec = pl.BlockSpec((3, tf), lambda j, i: (0, j))
    bspec = pl.BlockSpec((1, tf), lambda j, i: (0, j))
    if not bwd:
        in_specs, args = [main, prev, nxt, main, wspec, bspec], [ug, ug, ug, uv, cw, cb]
        out_shape, out_specs = jax.ShapeDtypeStruct((T, F), BF16), main
    else:
        in_specs = [main, prev, nxt] * 3 + [wspec, bspec]
        args = [ug, ug, ug, uv, uv, uv, da, da, da, cw, cb]
        out_shape = (jax.ShapeDtypeStruct((T, F), BF16), jax.ShapeDtypeStruct((T, F), BF16),
                     jax.ShapeDtypeStruct((3, F), F32), jax.ShapeDtypeStruct((1, F), F32))
        out_specs = (main, main, wspec, bspec)
    return pl.pallas_call(
        body, name=name, grid=(F // tf, T // tb), in_specs=in_specs, out_specs=out_specs, out_shape=out_shape,
        compiler_params=_params("parallel", "arbitrary" if bwd else "parallel"),
    )(*args)


def make_convgate(seq):
    @jax.custom_vjp
    def op(ug, uv, cw, cb):
        return _conv_call(ug, uv, cw, cb, None, seq=seq, name="conv_fwd")

    def fwd(ug, uv, cw, cb):
        return op(ug, uv, cw, cb), (ug, uv, cw, cb)

    def bwd(res, da):
        ug, uv, cw, cb = res
        return _conv_call(ug, uv, cw, cb, da, seq=seq, name="conv_bwd")

    op.defvjp(fwd, bwd)
    return op


def _attn_tiles(q_len, k_len, bwd):
    if bwd:
        return _pick(q_len, (1024, 512, 256, 128)), _pick(k_len, (768, 512, 256, 128))
    return _pick(q_len, (512, 256, 128)), _pick(k_len, (1408, 768, 512, 256, 128))


def _rows_spec(hm, t, d, head, blk):
    if hm:
        return pl.BlockSpec((None, t, d), lambda h, g, i, j: (head(h, g), blk(i, j), 0))
    return pl.BlockSpec((t, d), lambda h, g, i, j: (blk(i, j), head(h, g)))


def _cols_spec(hm, t, d, head, blk):
    if hm:
        return pl.BlockSpec((None, d, t), lambda h, g, i, j: (head(h, g), 0, blk(i, j)))
    return pl.BlockSpec((d, t), lambda h, g, i, j: (head(h, g), blk(i, j)))


LOG2E = 1.4426950408889634


def _attn_fwd_call(qT, k, vT, prev, *, hm, Hk, G, d, dv, scale, q0, q_len, k0, k_len, total, name):
    tq, tk = _attn_tiles(q_len, k_len, False)
    nq, nk, H = q_len // tq, k_len // tk, Hk * G
    qoff, koff = q0 // tq, k0 // tk
    c2 = scale * LOG2E

    def body(*refs):
        qT_ref, k_ref, vT_ref = refs[:3]
        oT_ref, lse_ref, m_s, l_s, acc_s = refs[-5:]
        kj = pl.program_id(3)

        @pl.when(kj == 0)
        def _():
            m_s[...] = jnp.full_like(m_s, -jnp.inf)
            l_s[...] = jnp.zeros_like(l_s)
            acc_s[...] = jnp.zeros_like(acc_s)

        sT = jnp.dot(k_ref[...], qT_ref[...], preferred_element_type=F32)
        m_prev = m_s[...]
        m_new = jnp.maximum(m_prev, jnp.max(sT, axis=0, keepdims=True))
        alpha = jnp.exp2((m_prev - m_new) * c2)
        pT = jnp.exp2((sT - m_new) * c2)
        l_s[...] = alpha * l_s[...] + jnp.sum(pT, axis=0, keepdims=True)
        acc_s[...] = alpha * acc_s[...] + jnp.dot(vT_ref[...], pT.astype(BF16), preferred_element_type=F32)
        m_s[...] = m_new

        @pl.when(kj == nk - 1)
        def _():
            oT_ref[...] = (acc_s[...] / l_s[...]).astype(oT_ref.dtype)
            lse_ref[...] = m_s[...] * scale + jnp.log(l_s[...])

    qh, kh = (lambda h, g: h * G + g), (lambda h, g: h)
    qb, kb = (lambda i, j: qoff + i), (lambda i, j: koff + j)
    in_specs = [_cols_spec(hm, tq, d, qh, qb), _rows_spec(hm, tk, d, kh, kb), _cols_spec(False, tk, dv, kh, kb)]
    args, alias = [qT, k, vT], {}
    if prev is not None:
        in_specs += [HBM_SPEC, HBM_SPEC]
        args += list(prev)
        alias = {3: 0, 4: 1}
    return pl.pallas_call(
        body, name=name, grid=(Hk, G, nq, nk), in_specs=in_specs,
        out_specs=(_cols_spec(False, tq, dv, qh, qb), pl.BlockSpec((None, 1, tq), lambda h, g, i, j: (h * G + g, 0, qoff + i))),
        out_shape=(jax.ShapeDtypeStruct((H * dv, total), BF16), jax.ShapeDtypeStruct((H, 1, total), F32)),
        scratch_shapes=[pltpu.VMEM((1, tq), F32), pltpu.VMEM((1, tq), F32), pltpu.VMEM((dv, tq), F32)],
        input_output_aliases=alias,
        compiler_params=_params("parallel", "parallel", "parallel", "arbitrary"),
    )(*args)


def _attn_bwd_call(qT, q, k, kT, v, doT, do, oT, lse, prev, *, hm, Hk, G, d, dv, scale, q0, q_len, k0, k_len, total, name):
    tq, tk = _attn_tiles(q_len, k_len, True)
    nq, nk, H = q_len // tq, k_len // tk, Hk * G
    qoff, koff = q0 // tq, k0 // tk
    c2 = scale * LOG2E
    acc_in = prev is not None

    def body(*refs):
        qT_ref, q_ref, k_ref, kT_ref, v_ref, doT_ref, do_ref, oT_ref, lse_ref = refs[:9]
        dqT_ref, dk_ref, dv_ref, dq_s, delta_s = refs[-5:]
        g, qi, kj = pl.program_id(1), pl.program_id(2), pl.program_id(3)

        @pl.when(jnp.logical_and(jnp.logical_and(g == 0, qi == 0), kj == 0))
        def _():
            if acc_in:
                dk_ref[...] = refs[10][...]
                dv_ref[...] = refs[11][...]
            else:
                dk_ref[...] = jnp.zeros_like(dk_ref)
                dv_ref[...] = jnp.zeros_like(dv_ref)

        doT_v = doT_ref[...]

        @pl.when(kj == 0)
        def _():
            delta_s[...] = jnp.sum(doT_v.astype(F32) * oT_ref[...].astype(F32), axis=0, keepdims=True)
            dq_s[...] = jnp.zeros_like(dq_s)

        sT = jnp.dot(k_ref[...], qT_ref[...], preferred_element_type=F32)
        pT = jnp.exp2(sT * c2 - lse_ref[...] * LOG2E)
        rows = pl.ds(pl.multiple_of(kj * tk, tk), tk)
        dv_ref[rows, :] += jnp.dot(pT.astype(BF16), do_ref[...], preferred_element_type=F32)
        dpT = jnp.dot(v_ref[...], doT_v, preferred_element_type=F32)
        dsT = (pT * (dpT - delta_s[...]) * scale).astype(BF16)
        dk_ref[rows, :] += jnp.dot(dsT, q_ref[...], preferred_element_type=F32)
        dq_s[...] += jnp.dot(kT_ref[...], dsT, preferred_element_type=F32)

        @pl.when(kj == nk - 1)
        def _():
            dqT_ref[...] = dq_s[...].astype(dqT_ref.dtype)

    qh, kh = (lambda h, g: h * G + g), (lambda h, g: h)
    qb, kb = (lambda i, j: qoff + i), (lambda i, j: koff + j)
    kres = lambda i, j: k0 // k_len
    in_specs = [_cols_spec(hm, tq, d, qh, qb), _rows_spec(hm, tq, d, qh, qb), _rows_spec(hm, tk, d, kh, kb),
                _cols_spec(hm, tk, d, kh, kb), _rows_spec(False, tk, dv, kh, kb), _cols_spec(False, tq, dv, qh, qb),
                _rows_spec(False, tq, dv, qh, qb), _cols_spec(False, tq, dv, qh, qb),
                pl.BlockSpec((None, 1, tq), lambda h, g, i, j: (h * G + g, 0, qoff + i))]
    args, alias = [qT, q, k, kT, v, doT, do, oT, lse], {}
    dk_spec, dv_spec = _rows_spec(hm, k_len, d, kh, kres), _rows_spec(False, k_len, dv, kh, kres)
    if acc_in:
        in_specs += [HBM_SPEC, dk_spec, dv_spec]
        args += list(prev)
        alias = {9: 0, 10: 1, 11: 2}
    dq_shape = (H, d, total) if hm else (H * d, total)
    dk_shape = (Hk, total, d) if hm else (total, Hk * d)
    return pl.pallas_call(
        body, name=name, grid=(Hk, G, nq, nk), in_specs=in_specs,
        out_specs=(_cols_spec(hm, tq, d, qh, qb), dk_spec, dv_spec),
        out_shape=(jax.ShapeDtypeStruct(dq_shape, BF16), jax.ShapeDtypeStruct(dk_shape, F32),
                   jax.ShapeDtypeStruct((total, Hk * dv), F32)),
        scratch_shapes=[pltpu.VMEM((d, tq), F32), pltpu.VMEM((1, tq), F32)],
        input_output_aliases=alias,
        compiler_params=_params("parallel", "arbitrary", "arbitrary", "arbitrary"),
    )(*args)


def make_attention(*, hm, Hk, G, d, dv, scale, seq, total, name):
    kw = dict(hm=hm, Hk=Hk, G=G, d=d, dv=dv, scale=scale, total=total)
    lat = dict(q0=0, q_len=seq, k0=0, k_len=total)
    ctx = dict(q0=seq, q_len=total - seq, k0=seq, k_len=total - seq)
    tr = (lambda a: jnp.swapaxes(a, 1, 2)) if hm else (lambda a: a.T)

    def run(q, k, v):
        qT, vT = tr(q), v.T
        first = _attn_fwd_call(qT, k, vT, None, name=name + "_fwd_lat", **kw, **lat)
        return _attn_fwd_call(qT, k, vT, first, name=name + "_fwd_ctx", **kw, **ctx)

    @jax.custom_vjp
    def op(q, k, v):
        return run(q, k, v)[0]

    def fwd(q, k, v):
        oT, lse = run(q, k, v)
        return oT, (q, k, v, oT, lse)

    def bwd(res, doT):
        q, k, v, oT, lse = res
        ops = (tr(q), q, k, tr(k), v, doT, doT.T, oT, lse)
        first = _attn_bwd_call(*ops, None, name=name + "_bwd_lat", **kw, **lat)
        dqT, dk, dv_ = _attn_bwd_call(*ops, first, name=name + "_bwd_ctx", **kw, **ctx)
        return tr(dqT).astype(q.dtype), dk.astype(k.dtype), dv_.astype(v.dtype)

    op.defvjp(fwd, bwd)
    return op


def _loss_call(xf, target):
    S, D = target.shape
    tb = _pick(S, ROW_TILES)

    def body(y_ref, t_ref, dy_ref, acc_ref):
        @pl.when(pl.program_id(0) == 0)
        def _():
            acc_ref[...] = jnp.zeros_like(acc_ref)

        e = y_ref[...] - t_ref[...]
        dy_ref[...] = e * (1.0 / D)
        acc_ref[...] += jnp.sum(e * e, axis=0, keepdims=True)

    row = pl.BlockSpec((tb, D), lambda i: (i, 0))
    return pl.pallas_call(
        body, name="loss", grid=(S // tb,), in_specs=[row, row],
        out_specs=(row, pl.BlockSpec((1, D), lambda i: (0, 0))),
        out_shape=(jax.ShapeDtypeStruct((S, D), F32), jax.ShapeDtypeStruct((1, D), F32)),
        compiler_params=_params("arbitrary"),
    )(xf, target)


def _adamw_call(w, g, m, v, name):
    R, N = w.shape
    tb = _pick(R, tuple(t for t in (512, 256, 128, 64, 32, 16, 8) if t * N * 4 <= (1 << 20)) or (8,))
    bc1, bc2 = 1.0 - ADAM_B1 ** ADAM_STEP, 1.0 - ADAM_B2 ** ADAM_STEP

    def body(w_ref, g_ref, m_ref, v_ref, d_ref, nm_ref, nv_ref):
        gv = g_ref[...]
        nm = ADAM_B1 * m_ref[...] + (1.0 - ADAM_B1) * gv
        nv = ADAM_B2 * v_ref[...] + (1.0 - ADAM_B2) * (gv * gv)
        d_ref[...] = -ADAM_LR * ((nm / bc1) / (jnp.sqrt(nv / bc2) + ADAM_EPS) + ADAM_WD * w_ref[...])
        nm_ref[...] = nm
        nv_ref[...] = nv

    spec = pl.BlockSpec((tb, N), lambda i: (i, 0))
    shp = jax.ShapeDtypeStruct((R, N), F32)
    return pl.pallas_call(
        body, name=name, grid=(R // tb,), in_specs=[spec] * 4, out_specs=(spec,) * 3, out_shape=(shp,) * 3,
        compiler_params=_params("parallel"),
    )(w, g, m, v)


def _adamw(w, g, m, v, name="adamw"):
    shape = w.shape
    two = (lambda a: a.reshape(-1, shape[-1])) if w.ndim >= 2 else (lambda a: a.reshape(1, -1))
    return tuple(r.reshape(shape) for r in _adamw_call(two(w), two(g), two(m), two(v), name))


def _me():
    return lax.axis_index("x"), lax.axis_index("y"), lax.axis_index("c")


def _gather8(x, *, reduce, name):
    R, W = x.shape

    def body(x_ref, out_ref, *scratch):
        if reduce:
            buf, send_sems, recv_sems, local_sem = scratch
        else:
            buf = out_ref
            send_sems, recv_sems, local_sem = scratch
        x, y, c = _me()
        me, sibling = (x, y, c), (x, y, 1 - c)
        chips = [(1 - x, y), (x, 1 - y), (1 - x, 1 - y)]

        def rows(px, py, pc):
            return buf.at[pl.ds((4 * px + 2 * py + pc) * R, R), :]

        def copy(k, block, to, src=None):
            return pltpu.make_async_remote_copy(
                src_ref=rows(*block) if src is None else src, dst_ref=rows(*block),
                send_sem=send_sems.at[k], recv_sem=recv_sems.at[k], device_id=to, device_id_type=MESH)

        mine = pltpu.make_async_copy(x_ref, rows(*me), local_sem)
        mine.start()
        first = [copy(0, me, sibling, src=x_ref)]
        first += [copy(1 + j, me, (*chip, c), src=x_ref) for j, chip in enumerate(chips)]
        for cp in first:
            cp.start()
        passed = [copy(4 + j, (*chip, c), sibling) for j, chip in enumerate(chips)]
        for j, chip in enumerate(chips):
            copy(1 + j, (*chip, c), me).wait_recv()
            passed[j].start()
        copy(0, sibling, me).wait_recv()
        for j, chip in enumerate(chips):
            copy(4 + j, (*chip, 1 - c), me).wait_recv()
        for cp in first + passed:
            cp.wait_send()
        mine.wait()
        if reduce:
            acc = buf[0:R, :]
            for e in range(1, 8):
                acc = acc + buf[e * R:(e + 1) * R, :]
            out_ref[...] = acc

    sems = [pltpu.SemaphoreType.DMA((7,)), pltpu.SemaphoreType.DMA((7,)), pltpu.SemaphoreType.DMA]
    return pl.pallas_call(
        body, name=name,
        out_shape=jax.ShapeDtypeStruct((R if reduce else 8 * R, W), F32),
        in_specs=[pl.BlockSpec(memory_space=pltpu.VMEM)], out_specs=pl.BlockSpec(memory_space=pltpu.VMEM),
        scratch_shapes=([pltpu.VMEM((8 * R, W), F32)] if reduce else []) + sems,
        compiler_params=pltpu.CompilerParams(vmem_limit_bytes=VMEM_LIMIT),
    )(x)


def _small_exchange(flat, *, reduce, name):
    n = flat.shape[0]
    W = PACK_W if n >= 8 * PACK_W else LANES
    R = -(-n // (8 * W)) * 8
    x = jnp.pad(flat, (0, R * W - n)).reshape(R, W)
    out = _gather8(x, reduce=reduce, name=name)
    if reduce:
        return out.reshape(-1)[:n]
    return out.reshape(8, R * W)[:, :n]


HBM_SPEC = pl.BlockSpec(memory_space=pl.ANY)


def _chip_gather(mine):
    _, R, W = mine.shape

    def body(m_ref, out_ref, send_sems, recv_sems):
        x, y, c = _me()
        sibling = (x, y, 1 - c)
        chips = [(1 - x, y), (x, 1 - y), (1 - x, 1 - y)]

        def slot(px, py, half):
            return out_ref.at[half, 2 * px + py]

        def copy(k, src, dst, to):
            return pltpu.make_async_remote_copy(src_ref=src, dst_ref=dst, send_sem=send_sems.at[k],
                                                recv_sem=recv_sems.at[k], device_id=to, device_id_type=MESH)

        first = [copy(j, m_ref.at[c], slot(x, y, c), (*chip, c)) for j, chip in enumerate(chips)]
        for cp in first:
            cp.start()
        passed = [copy(3 + j, slot(*chip, c), slot(*chip, c), sibling) for j, chip in enumerate(chips)]
        for j, chip in enumerate(chips):
            copy(j, m_ref.at[c], slot(*chip, c), (*chip, c)).wait_recv()
            passed[j].start()
        for j, chip in enumerate(chips):
            copy(3 + j, slot(*chip, 1 - c), slot(*chip, 1 - c), sibling).wait_recv()
        for cp in first + passed:
            cp.wait_send()

    out = pl.pallas_call(
        body, name="chip_gather", out_shape=jax.ShapeDtypeStruct((2, 4, R, W), mine.dtype),
        in_specs=[HBM_SPEC], out_specs=HBM_SPEC,
        scratch_shapes=[pltpu.SemaphoreType.DMA((6,)), pltpu.SemaphoreType.DMA((6,))],
    )(mine)
    x, y, _ = _me()
    return lax.dynamic_update_slice(out, mine[:, None], (0, 2 * x + y, 0, 0))


def _pair_swap(buf):
    def body(b_ref, out_ref, send_sem, recv_sem):
        x, y, c = _me()
        cp = pltpu.make_async_remote_copy(src_ref=b_ref.at[1 - c], dst_ref=out_ref, send_sem=send_sem,
                                          recv_sem=recv_sem, device_id=(x, y, 1 - c), device_id_type=MESH)
        cp.start()
        cp.wait()

    return pl.pallas_call(
        body, name="pair_swap_grads", out_shape=jax.ShapeDtypeStruct(buf.shape[1:], buf.dtype), in_specs=[HBM_SPEC],
        out_specs=HBM_SPEC, scratch_shapes=[pltpu.SemaphoreType.DMA, pltpu.SemaphoreType.DMA],
    )(buf)


def _pair_share(half):
    def body(h_ref, out_ref, send_sem, recv_sem):
        x, y, c = _me()
        cp = pltpu.make_async_remote_copy(src_ref=h_ref, dst_ref=out_ref.at[c], send_sem=send_sem, recv_sem=recv_sem,
                                          device_id=(x, y, 1 - c), device_id_type=MESH)
        cp.start()
        pltpu.make_async_remote_copy(src_ref=h_ref, dst_ref=out_ref.at[1 - c], send_sem=send_sem, recv_sem=recv_sem,
                                     device_id=(x, y, 1 - c), device_id_type=MESH).wait_recv()
        cp.wait_send()

    out = pl.pallas_call(
        body, name="pair_share", out_shape=jax.ShapeDtypeStruct((2,) + half.shape, half.dtype), in_specs=[HBM_SPEC],
        out_specs=HBM_SPEC, scratch_shapes=[pltpu.SemaphoreType.DMA, pltpu.SemaphoreType.DMA],
    )(half)
    return lax.dynamic_update_slice(out, half[None], (lax.axis_index("c"), 0, 0))


def _chip_exchange(s):
    def body(s_ref, out_ref, send_sems, recv_sems):
        x, y, c = _me()
        q = 2 * x + y
        chips = [(1 - x, y), (x, 1 - y), (1 - x, 1 - y)]

        def copy(j, chip):
            return pltpu.make_async_remote_copy(
                src_ref=s_ref.at[2 * chip[0] + chip[1]], dst_ref=out_ref.at[q], send_sem=send_sems.at[j],
                recv_sem=recv_sems.at[j], device_id=(*chip, c), device_id_type=MESH)

        sends = [copy(j, chip) for j, chip in enumerate(chips)]
        for cp in sends:
            cp.start()
        for j, chip in enumerate(chips):
            pltpu.make_async_remote_copy(
                src_ref=s_ref.at[q], dst_ref=out_ref.at[2 * chip[0] + chip[1]], send_sem=send_sems.at[j],
                recv_sem=recv_sems.at[j], device_id=(*chip, c), device_id_type=MESH).wait_recv()
        for cp in sends:
            cp.wait_send()

    return pl.pallas_call(
        body, name="chip_exchange", out_shape=jax.ShapeDtypeStruct(s.shape, s.dtype), in_specs=[HBM_SPEC],
        out_specs=HBM_SPEC,
        scratch_shapes=[pltpu.SemaphoreType.DMA((3,)), pltpu.SemaphoreType.DMA((3,))],
    )(s)


def _pair_sum(p, got, c):
    _, _, R, W = p.shape
    tb = _pick(R, ROW_TILES)

    def body(c_ref, p_ref, g_ref, o_ref):
        o_ref[...] = (p_ref[...].astype(F32) + g_ref[...].astype(F32)).astype(o_ref.dtype)

    return pl.pallas_call(
        body, name="pair_sum",
        grid_spec=pltpu.PrefetchScalarGridSpec(
            num_scalar_prefetch=1, grid=(4, R // tb),
            in_specs=[pl.BlockSpec((None, None, tb, W), lambda s, i, cr: (cr[0], s, i, 0)),
                      pl.BlockSpec((None, tb, W), lambda s, i, cr: (s, i, 0))],
            out_specs=pl.BlockSpec((None, tb, W), lambda s, i, cr: (s, i, 0))),
        out_shape=jax.ShapeDtypeStruct((4, R, W), BF16),
        compiler_params=_params("parallel", "parallel"),
    )(jnp.reshape(c, (1,)).astype(jnp.int32), p, got)


def _sum4(r, s, q):
    _, R, W = r.shape
    tb = _pick(R, ROW_TILES)

    def body(q_ref, r_ref, s_ref, o_ref):
        own = s_ref[...].astype(F32)
        acc = jnp.where(q_ref[0] == 0, own, r_ref[0].astype(F32))
        for a in range(1, 4):
            acc = acc + jnp.where(q_ref[0] == a, own, r_ref[a].astype(F32))
        o_ref[...] = acc

    return pl.pallas_call(
        body, name="sum4",
        grid_spec=pltpu.PrefetchScalarGridSpec(
            num_scalar_prefetch=1, grid=(R // tb,),
            in_specs=[pl.BlockSpec((4, tb, W), lambda i, qr: (0, i, 0)),
                      pl.BlockSpec((None, tb, W), lambda i, qr: (qr[0], i, 0))],
            out_specs=pl.BlockSpec((tb, W), lambda i, qr: (i, 0))),
        out_shape=jax.ShapeDtypeStruct((R, W), F32),
        compiler_params=_params("parallel"),
    )(jnp.reshape(q, (1,)).astype(jnp.int32), r, s)


BIG = (("mla_w_dq", 1), ("mla_w_uq", 2), ("mla_w_dkv", 1), ("mla_w_ukv", 2), ("mla_w_o", 1),
       ("gqa_w_q", 1), ("gqa_w_kv", 1), ("gqa_w_o", 1), ("ffn_w_up", 2), ("ffn_w_down", 1))


def _pack_rows(n):
    return -(-n // (2 * PACK_W * ROW_TILES[0])) * ROW_TILES[0]


def _to_shards(full, axis):
    L, K, N = full.shape
    if axis == 1:
        return full.reshape(L, 4, K // 4, N).transpose(1, 0, 2, 3).reshape(4, -1)
    return full.reshape(L, K, 4, N // 4).transpose(2, 0, 1, 3).reshape(4, -1)


def _from_shards(flat4, shard_shape, axis):
    L, K, N = shard_shape
    a = flat4.reshape(4, L, K, N)
    if axis == 1:
        return a.transpose(1, 0, 2, 3).reshape(L, 4 * K, N)
    return a.transpose(1, 2, 0, 3).reshape(L, K, 4 * N)


def _rope_tables(seq, total, rot_dim):
    t = jnp.arange(seq, dtype=jnp.int32)
    rows, cols = t // GRID_W, t % GRID_W
    axis_dim = rot_dim // 2
    inv = jnp.power(ROPE_BASE, -jnp.arange(0, axis_dim, 2, dtype=F32) / axis_dim)
    ang_r = rows.astype(F32)[:, None] * inv
    ang_c = cols.astype(F32)[:, None] * inv
    ang = jnp.concatenate([ang_r, ang_r, ang_c, ang_c], axis=-1)
    cos = jnp.concatenate([jnp.cos(ang), jnp.ones((total - seq, rot_dim), F32)], axis=0)
    sin = jnp.concatenate([jnp.sin(ang), jnp.zeros((total - seq, rot_dim), F32)], axis=0)
    rep = LANES // rot_dim
    return jnp.tile(cos, (1, rep)), jnp.tile(sin, (1, rep))


def _forward(X, mod, P, *, seq):
    T, D = X.shape
    depth = mod.shape[0]
    mask = (jnp.arange(T) >= seq).astype(F32)[:, None]
    rope_m = _rope_tables(seq, T, ROPE_D)
    rope_g = _rope_tables(seq, T, HEAD)
    Hm = P["mla_w_uq"].shape[-1] // (NOPE + ROPE_D)
    Hg = P["gqa_w_q"].shape[-1] // HEAD
    Hkv = P["gqa_w_kv"].shape[-1] // (2 * HEAD)
    kv_rank = P["mla_w_dkv"].shape[-1] - ROPE_D
    dff = P["ffn_conv_b"].shape[-1]

    norm_row = make_norm(P["mla_w_dq"].shape[-1], name="norm_rank")
    norm_kv = make_norm(kv_rank, name="norm_kvrank")
    norm_head = make_norm(LANES, name="norm_head")
    norm_pe = make_norm(ROPE_D, ROPE_D // 4, name="norm_pe")
    norm_gqa = make_norm(HEAD, HEAD // 4, name="norm_gqa")
    attn_mla = make_attention(hm=True, Hk=Hm, G=1, d=NOPE + ROPE_D, dv=VDIM, scale=1.0 / math.sqrt(NOPE + ROPE_D),
                              seq=seq, total=T, name="mla")
    attn_gqa = make_attention(hm=False, Hk=Hkv, G=Hg // Hkv, d=HEAD, dv=HEAD, scale=1.0 / math.sqrt(HEAD),
                              seq=seq, total=T, name="gqa")
    convgate = make_convgate(seq)
    row = lambda g: g.reshape(1, -1)
    twice = lambda g: jnp.concatenate([g, g]).reshape(1, -1)

    for i in range(depth):
        j = i // 2
        sh1, sc1, g1, sh2, sc2, g2 = [mod[i][:, k * D:(k + 1) * D] for k in range(N_MOD)]
        h = modnorm(X, row(P["norm_mix"][i]), sh1, sc1, mask)
        if i % 2 == 0:
            w_uq = P["mla_w_uq"][j].reshape(-1, Hm, NOPE + ROPE_D)
            w_ukv = P["mla_w_ukv"][j].reshape(-1, Hm, NOPE + VDIM)
            cq = norm_row(linear(h, P["mla_w_dq"][j]), row(P["mla_g_dq"][j]), None, None)
            qn = norm_head(linear(cq, w_uq[:, :, :NOPE].reshape(-1, Hm * NOPE)), row(P["mla_g_q_nope"][j]), None, None)
            qp = norm_pe(linear(cq, w_uq[:, :, NOPE:].reshape(-1, Hm * ROPE_D)), twice(P["mla_g_q_pe"][j]), *rope_m)
            kva = linear(h, P["mla_w_dkv"][j])
            ckv = norm_kv(kva[:, :kv_rank], row(P["mla_g_dkv"][j]), None, None)
            kp_raw = kva[:, kv_rank:]
            kp = norm_pe(jnp.concatenate([kp_raw, kp_raw], axis=-1), twice(P["mla_g_k_pe"][j]), *rope_m)[:, :ROPE_D]
            kn = norm_head(linear(ckv, w_ukv[:, :, :NOPE].reshape(-1, Hm * NOPE)), row(P["mla_g_k_nope"][j]), None, None)
            v = linear_bf16(ckv, w_ukv[:, :, NOPE:].reshape(-1, Hm * VDIM))
            q_hm = jnp.concatenate([qn.reshape(T, Hm, NOPE), qp.reshape(T, Hm, ROPE_D)], axis=-1).transpose(1, 0, 2)
            k_hm = jnp.concatenate([kn.reshape(T, Hm, NOPE), jnp.broadcast_to(kp[:, None, :], (T, Hm, ROPE_D))],
                                   axis=-1).transpose(1, 0, 2)
            oT = attn_mla(q_hm, k_hm, v)
            X = linear_gated_t(oT, P["mla_w_o"][j], X, g1, mask)
        else:
            w_kv = P["gqa_w_kv"][j]
            q = norm_gqa(linear(h, P["gqa_w_q"][j]), row(P["gqa_g_q"][j]), *rope_g)
            k = norm_gqa(linear(h, w_kv[:, :Hkv * HEAD]), row(P["gqa_g_k"][j]), *rope_g)
            v = linear_bf16(h, w_kv[:, Hkv * HEAD:])
            oT = attn_gqa(q, k, v)
            X = linear_gated_t(oT, P["gqa_w_o"][j], X, g1, mask)
        h2 = modnorm(X, row(P["norm_ffn"][i]), sh2, sc2, mask)
        w_up = P["ffn_w_up"][i]
        ug = linear_bf16(h2, w_up[:, :dff])
        uv = linear_bf16(h2, w_up[:, dff:])
        a = convgate(ug, uv, P["ffn_conv_w"][i], row(P["ffn_conv_b"][i]))
        X = linear_gated(a, P["ffn_w_down"][i], X, g2, mask)
    return X


SMALL = ("norm_mix", "norm_ffn", "mla_g_dq", "mla_g_q_nope", "mla_g_q_pe", "mla_g_dkv", "mla_g_k_pe",
         "mla_g_k_nope", "gqa_g_q", "gqa_g_k", "ffn_conv_w", "ffn_conv_b")
WEIGHTS = ("c_ctx", "w_mod", "b_mod", "norm_mix", "norm_ffn", "mla_w_dq", "mla_g_dq", "mla_w_uq", "mla_g_q_nope",
           "mla_g_q_pe", "mla_w_dkv", "mla_g_dkv", "mla_g_k_pe", "mla_w_ukv", "mla_g_k_nope", "mla_w_o", "gqa_w_q",
           "gqa_g_q", "gqa_w_kv", "gqa_g_k", "gqa_w_o", "ffn_w_up", "ffn_conv_w", "ffn_conv_b", "ffn_w_down")


def _step(A):
    x, y, c = _me()
    q = 2 * x + y
    dev = 4 * x + 2 * y + c
    seq, D = A["x"].shape[1], A["x"].shape[2]
    depth = A["w_mod"].shape[0]
    X0 = jnp.concatenate([A["x"][0], A["ctx"][0]], axis=0)

    shard_n = [math.prod(A[n].shape) for n, _ in BIG]
    n_tot = sum(shard_n)
    R = _pack_rows(n_tot)
    flat = jnp.concatenate([A[n].astype(BF16).reshape(-1) for n, _ in BIG])
    mine = jnp.pad(flat, (0, 2 * R * PACK_W - n_tot)).reshape(2, R, PACK_W)
    allw = _chip_gather(mine).transpose(1, 0, 2, 3).reshape(4, -1)
    P, off = {}, 0
    for (n, axis), cnt in zip(BIG, shard_n):
        P[n] = _from_shards(allw[:, off:off + cnt], A[n].shape, axis)
        off += cnt

    c_all = _small_exchange(A["c"].reshape(-1), reduce=False, name="gather_c")
    cw = _small_exchange(A["ffn_conv_w"].reshape(-1), reduce=False, name="gather_convw")
    cw = cw.reshape(4, 2, depth, 3, -1)[:, 0].transpose(1, 2, 0, 3).reshape(depth, 3, -1)
    cvec = jnp.concatenate([c_all, jnp.broadcast_to(A["c_ctx"][None, :], (8, D))], axis=0)
    act = jax.nn.silu(cvec)
    ncol = A["w_mod"].shape[-1]
    modcols = jnp.stack([_mm(act, A["w_mod"][i], name="mod_fwd") for i in range(depth)])
    mod_all = _small_exchange(modcols.reshape(-1), reduce=False, name="gather_mod")
    mod_all = mod_all.reshape(4, 2, depth, 16, ncol)[:, 0].transpose(1, 2, 0, 3).reshape(depth, 16, 4 * ncol)
    mod_all = mod_all + A["b_mod"][:, None, :]
    mod = jnp.stack([lax.dynamic_index_in_dim(mod_all, dev, 1, keepdims=False), mod_all[:, 8]], axis=1)

    for n in SMALL:
        P[n] = cw if n == "ffn_conv_w" else A[n]

    XF, vjp = jax.vjp(functools.partial(_forward, seq=seq), X0, mod, P)
    dy, sq = _loss_call(XF, A["loss_target"][0])
    loss = lax.psum(0.5 / D * jnp.sum(sq), ("x", "y", "c"))
    dX0, dmod, dP = vjp(jnp.concatenate([dy, jnp.zeros((X0.shape[0] - seq, D), F32)], axis=0))
    grad_x = dX0[:seq][None]

    G = {}
    dmod_all = _small_exchange(dmod.reshape(-1), reduce=False, name="gather_dmod").reshape(8, depth, 2, -1)
    drows = jnp.concatenate([dmod_all[:, :, 0], dmod_all[:, :, 1]], axis=0).transpose(1, 0, 2)
    G["b_mod"] = jnp.sum(drows, axis=1)
    dcols = lax.dynamic_slice_in_dim(drows, q * ncol, ncol, axis=2)
    G["w_mod"] = jnp.stack([_mm(act, dcols[i], ta=True, name="mod_dw") for i in range(depth)])
    dact = sum(_mm(dcols[i], A["w_mod"][i], tb=True, name="mod_dact") for i in range(depth))
    dcc_part = 0.5 * jnp.sum(dact[8:], axis=0)

    small_list = [dP[n].astype(F32).reshape(-1) for n in SMALL] + [dcc_part]
    sizes = [a.shape[0] for a in small_list]
    tot = _small_exchange(jnp.concatenate(small_list), reduce=True, name="reduce_small")
    off = 0
    for n, cnt in zip(SMALL + ("c_ctx",), sizes):
        G[n] = tot[off:off + cnt].reshape(P[n].shape if n != "c_ctx" else (D,))
        off += cnt
    sg = jax.nn.sigmoid(A["c_ctx"])
    G["c_ctx"] = G["c_ctx"] * (sg * (1.0 + A["c_ctx"] * (1.0 - sg)))
    fcols = A["ffn_conv_w"].shape[-1]
    G["ffn_conv_w"] = lax.dynamic_slice_in_dim(G["ffn_conv_w"], q * fcols, fcols, axis=2)

    gflat = jnp.concatenate([_to_shards(dP[n], axis) for n, axis in BIG], axis=1)
    gp = jnp.pad(gflat, ((0, 0), (0, 2 * R * PACK_W - n_tot))).reshape(4, 2, R, PACK_W).transpose(1, 0, 2, 3)
    got = _pair_swap(gp)
    pair = _pair_sum(gp, got, c)
    half = _sum4(_chip_exchange(pair), pair, q)
    both = _pair_share(half).reshape(-1)
    off = 0
    for (n, _), cnt in zip(BIG, shard_n):
        G[n] = both[off:off + cnt].reshape(A[n].shape)
        off += cnt

    delta, new_m, new_v = {}, {}, {}
    tiny = [n for n in WEIGHTS if A[n].size < (1 << 18)]
    cat = lambda pre: jnp.concatenate([(A[pre + n] if pre != "g" else G[n]).reshape(-1) for n in tiny])
    n_tiny = sum(A[n].size for n in tiny)
    rt = -(-n_tiny // (8 * PACK_W)) * 8
    shape2 = lambda a: jnp.pad(a, (0, rt * PACK_W - n_tiny)).reshape(rt, PACK_W)
    outs = _adamw_call(shape2(cat("")), shape2(cat("g")), shape2(cat("m_")), shape2(cat("v_")), "adamw_small")
    off = 0
    for n in tiny:
        cnt = A[n].size
        delta[n], new_m[n], new_v[n] = [o.reshape(-1)[off:off + cnt].reshape(A[n].shape) for o in outs]
        off += cnt
    for n in WEIGHTS:
        if n not in tiny:
            delta[n], new_m[n], new_v[n] = _adamw(A[n], G[n], A["m_" + n], A["v_" + n])
    return (loss, grad_x, *[G[n] for n in WEIGHTS], *[delta[n] for n in WEIGHTS],
            *[new_m[n] for n in WEIGHTS], *[new_v[n] for n in WEIGHTS])


def kernel(x, c, ctx, c_ctx, w_mod, b_mod, norm_mix, norm_ffn, mla_w_dq, mla_g_dq, mla_w_uq, mla_g_q_nope, mla_g_q_pe, mla_w_dkv, mla_g_dkv, mla_g_k_pe, mla_w_ukv, mla_g_k_nope, mla_w_o, gqa_w_q, gqa_g_q, gqa_w_kv, gqa_g_k, gqa_w_o, ffn_w_up, ffn_conv_w, ffn_conv_b, ffn_w_down, loss_target, m_c_ctx, m_w_mod, m_b_mod, m_norm_mix, m_norm_ffn, m_mla_w_dq, m_mla_g_dq, m_mla_w_uq, m_mla_g_q_nope, m_mla_g_q_pe, m_mla_w_dkv, m_mla_g_dkv, m_mla_g_k_pe, m_mla_w_ukv, m_mla_g_k_nope, m_mla_w_o, m_gqa_w_q, m_gqa_g_q, m_gqa_w_kv, m_gqa_g_k, m_gqa_w_o, m_ffn_w_up, m_ffn_conv_w, m_ffn_conv_b, m_ffn_w_down, v_c_ctx, v_w_mod, v_b_mod, v_norm_mix, v_norm_ffn, v_mla_w_dq, v_mla_g_dq, v_mla_w_uq, v_mla_g_q_nope, v_mla_g_q_pe, v_mla_w_dkv, v_mla_g_dkv, v_mla_g_k_pe, v_mla_w_ukv, v_mla_g_k_nope, v_mla_w_o, v_gqa_w_q, v_gqa_g_q, v_gqa_w_kv, v_gqa_g_k, v_gqa_w_o, v_ffn_w_up, v_ffn_conv_w, v_ffn_conv_b, v_ffn_w_down):
    return _step(dict(locals()))
```

```python
import functools
import math

import jax
import jax.numpy as jnp
from jax import lax
from jax.experimental import pallas as pl
from jax.experimental.pallas import tpu as pltpu

F32, BF16 = jnp.float32, jnp.bfloat16
MESH = pl.DeviceIdType.MESH

EPS = 1e-6
ROPE_BASE = 10000.0
GRID_W = 64
NOPE, ROPE_D, VDIM = 128, 64, 128
HEAD = 128
N_MOD = 6
LANES = 128
HALO = 16
VMEM_LIMIT = 56 * 1024 * 1024
PACK_W = 1024

ADAM_LR, ADAM_B1, ADAM_B2, ADAM_EPS, ADAM_WD, ADAM_STEP = 0.001, 0.9, 0.999, 1e-08, 0.01, 10

LANE_TILES = (1024, 768, 512, 256, 128)
K_RESIDENT = 2048
ROW_TILES = (256, 128, 64, 32, 16, 8)


def _pick(n, prefs):
    for p in prefs:
        if n % p == 0:
            return p
    return n


def _params(*sem):
    return pltpu.CompilerParams(dimension_semantics=sem, vmem_limit_bytes=VMEM_LIMIT)


def _mm(a, b, *, ta=False, tb=False, out_dtype=F32, resid=None, gate=None, mask=None, name="mm"):
    (K, M) = a.shape if ta else a.shape[::-1]
    N = b.shape[0] if tb else b.shape[1]
    assert (b.shape[1] if tb else b.shape[0]) == K
    tm, tn = _pick(M, LANE_TILES), _pick(N, LANE_TILES)
    tk = K if K <= K_RESIDENT else _pick(K, (1408,) + LANE_TILES)
    nk = K // tk
    fused = resid is not None
    dn = (((0 if ta else 1,), (1 if tb else 0,)), ((), ()))

    def body(*refs):
        refs = list(refs)
        acc_ref = refs.pop() if nk > 1 else None
        if fused:
            a_ref, b_ref, r_ref, g_ref, m_ref, o_ref, y_ref = refs
        else:
            a_ref, b_ref, o_ref = refs
        part = lax.dot_general(a_ref[...].astype(BF16), b_ref[...].astype(BF16), dn, preferred_element_type=F32)

        def finish(acc):
            if fused:
                g = jnp.where(m_ref[...] > 0.5, g_ref[1:2, :], g_ref[0:1, :])
                y_ref[...] = acc.astype(y_ref.dtype)
                o_ref[...] = r_ref[...] + g * acc
            else:
                o_ref[...] = acc.astype(o_ref.dtype)

        if nk == 1:
            finish(part)
        else:
            k = pl.program_id(2)

            @pl.when(k == 0)
            def _():
                acc_ref[...] = part

            @pl.when(k > 0)
            def _():
                acc_ref[...] += part

            @pl.when(k == nk - 1)
            def _():
                finish(acc_ref[...])

    a_spec = pl.BlockSpec((tk, tm), lambda i, j, k: (k, i)) if ta else pl.BlockSpec((tm, tk), lambda i, j, k: (i, k))
    b_spec = pl.BlockSpec((tn, tk), lambda i, j, k: (j, k)) if tb else pl.BlockSpec((tk, tn), lambda i, j, k: (k, j))
    o_spec = pl.BlockSpec((tm, tn), lambda i, j, k: (i, j))
    in_specs, args = [a_spec, b_spec], [a, b]
    out_shape, out_specs = jax.ShapeDtypeStruct((M, N), out_dtype), o_spec
    if fused:
        in_specs += [o_spec, pl.BlockSpec((2, tn), lambda i, j, k: (0, j)), pl.BlockSpec((tm, 1), lambda i, j, k: (i, 0))]
        args += [resid, gate, mask]
        out_shape = (jax.ShapeDtypeStruct((M, N), F32), jax.ShapeDtypeStruct((M, N), BF16))
        out_specs = (o_spec, o_spec)
    return pl.pallas_call(
        body, name=name, grid=(M // tm, N // tn, nk), in_specs=in_specs, out_specs=out_specs, out_shape=out_shape,
        scratch_shapes=[pltpu.VMEM((tm, tn), F32)] if nk > 1 else [],
        compiler_params=_params("parallel", "parallel", "arbitrary"),
    )(*args)


def _make_linear(out_dtype, name):
    @jax.custom_vjp
    def op(a, aT, w):
        return _mm(a, w, out_dtype=out_dtype, name=name)

    def fwd(a, aT, w):
        return op(a, aT, w), (aT, w)

    def bwd(res, dy):
        aT, w = res
        return (_mm(dy, w, tb=True, out_dtype=aT.dtype, name="lin_da"), None,
                _mm(aT, dy, out_dtype=w.dtype, name="lin_dw"))

    op.defvjp(fwd, bwd)
    return op


linear = _make_linear(F32, "lin_fwd")
linear_bf16 = _make_linear(BF16, "linb_fwd")


def _gate_bwd(dx, y, gate, mask):
    T, N = dx.shape
    tb = _pick(T, ROW_TILES)

    def body(dx_ref, y_ref, g_ref, m_ref, dy_ref, dg_ref):
        @pl.when(pl.program_id(0) == 0)
        def _():
            dg_ref[...] = jnp.zeros_like(dg_ref)

        d = dx_ref[...]
        ctx = m_ref[...] > 0.5
        dy_ref[...] = (jnp.where(ctx, g_ref[1:2, :], g_ref[0:1, :]) * d).astype(dy_ref.dtype)
        dyy = d * y_ref[...].astype(F32)
        dg_ref[0:1, :] += jnp.sum(jnp.where(ctx, 0.0, dyy), axis=0, keepdims=True)
        dg_ref[1:2, :] += jnp.sum(jnp.where(ctx, dyy, 0.0), axis=0, keepdims=True)

    row = pl.BlockSpec((tb, N), lambda i: (i, 0))
    small = pl.BlockSpec((2, N), lambda i: (0, 0))
    return pl.pallas_call(
        body, name="gate_bwd", grid=(T // tb,),
        in_specs=[row, row, small, pl.BlockSpec((tb, 1), lambda i: (i, 0))], out_specs=(row, small),
        out_shape=(jax.ShapeDtypeStruct((T, N), BF16), jax.ShapeDtypeStruct((2, N), F32)),
        compiler_params=_params("arbitrary"),
    )(dx, y, gate, mask)


@jax.custom_vjp
def linear_gated(a, aT, w, x, gate, mask):
    return _mm(a, w, resid=x, gate=gate, mask=mask, name="ling_fwd")[0]


def _linear_gated_fwd(a, aT, w, x, gate, mask):
    out, y = _mm(a, w, resid=x, gate=gate, mask=mask, name="ling_fwd")
    return out, (aT, w, y, gate, mask)


def _linear_gated_bwd(res, dx):
    aT, w, y, gate, mask = res
    dy, dgate = _gate_bwd(dx, y, gate, mask)
    return (_mm(dy, w, tb=True, out_dtype=aT.dtype, name="ling_da"), None,
            _mm(aT, dy, out_dtype=w.dtype, name="ling_dw"), dx, dgate, None)


linear_gated.defvjp(_linear_gated_fwd, _linear_gated_bwd)


def _rope_rot(x, quarter):
    lane = lax.broadcasted_iota(jnp.int32, x.shape, 1)
    lo = (lane % (2 * quarter)) < quarter
    return jnp.where(lo, -pltpu.roll(x, LANES - quarter, 1), pltpu.roll(x, quarter, 1))


def _group_mean(v, group):
    if group == LANES:
        return jnp.mean(v, axis=-1, keepdims=True)
    lane = lax.broadcasted_iota(jnp.int32, v.shape, 1)
    lo = lane < group
    s_lo = jnp.sum(jnp.where(lo, v, 0.0), axis=-1, keepdims=True)
    s_hi = jnp.sum(jnp.where(lo, 0.0, v), axis=-1, keepdims=True)
    return jnp.where(lo, s_lo, s_hi) * (1.0 / group)


def _norm_call(x, gain, dy, *, group, quarter, cos, sin, shift, scale, mask, out_dtype, name, dres=None):
    T, W = x.shape
    whole = group == W
    use_rope, use_mod, bwd = cos is not None, shift is not None, dy is not None
    assert not (whole and use_rope) and not (use_mod and not whole)
    tb = _pick(T, ROW_TILES)
    nchunk = W // LANES

    def body(*refs):
        refs = list(refs)
        x_ref, g_ref = refs.pop(0), refs.pop(0)
        dy_ref = refs.pop(0) if bwd else None
        dres_ref = refs.pop(0) if (bwd and use_mod) else None
        cos_ref, sin_ref = (refs.pop(0), refs.pop(0)) if use_rope else (None, None)
        sh_ref, sc_ref, m_ref = (refs.pop(0), refs.pop(0), refs.pop(0)) if use_mod else (None, None, None)
        if not bwd and use_mod:
            o_ref, oT_ref = refs
        elif not bwd:
            (o_ref,) = refs
        elif use_mod:
            dx_ref, dg_ref, dsh_ref, dsc_ref = refs
        else:
            dx_ref, dg_ref = refs

        if bwd:
            @pl.when(pl.program_id(0) == 0)
            def _():
                dg_ref[...] = jnp.zeros_like(dg_ref)
                if use_mod:
                    dsh_ref[...] = jnp.zeros_like(dsh_ref)
                    dsc_ref[...] = jnp.zeros_like(dsc_ref)

        if whole:
            xv = x_ref[...].astype(F32)
            r = lax.rsqrt(jnp.mean(xv * xv, axis=-1, keepdims=True) + EPS)
            xn = xv * r
            gain_v = g_ref[...]
            if use_mod:
                ctx = m_ref[...] > 0.5
                sc = jnp.where(ctx, sc_ref[1:2, :], sc_ref[0:1, :])
            if not bwd:
                y = xn * gain_v
                if use_mod:
                    y = y * (1.0 + sc) + jnp.where(ctx, sh_ref[1:2, :], sh_ref[0:1, :])
                o_ref[...] = y.astype(o_ref.dtype)
                if use_mod:
                    oT_ref[...] = y.T.astype(oT_ref.dtype)
            else:
                d = dy_ref[...].astype(F32)
                if use_mod:
                    dyy = d * (xn * gain_v)
                    dsh_ref[0:1, :] += jnp.sum(jnp.where(ctx, 0.0, d), axis=0, keepdims=True)
                    dsh_ref[1:2, :] += jnp.sum(jnp.where(ctx, d, 0.0), axis=0, keepdims=True)
                    dsc_ref[0:1, :] += jnp.sum(jnp.where(ctx, 0.0, dyy), axis=0, keepdims=True)
                    dsc_ref[1:2, :] += jnp.sum(jnp.where(ctx, dyy, 0.0), axis=0, keepdims=True)
                    d = d * (1.0 + sc)
                dg_ref[...] += jnp.sum(d * xn, axis=0, keepdims=True)
                dxn = d * gain_v
                dxv = r * (dxn - xn * jnp.mean(dxn * xn, axis=-1, keepdims=True))
                if use_mod:
                    dxv = dxv + dres_ref[...]
                dx_ref[...] = dxv.astype(dx_ref.dtype)
        else:
            gain_v = g_ref[...]
            if use_rope:
                cs, sn = cos_ref[...], sin_ref[...]
            for c in range(nchunk):
                cols = slice(c * LANES, (c + 1) * LANES)
                xv = x_ref[:, cols].astype(F32)
                r = lax.rsqrt(_group_mean(xv * xv, group) + EPS)
                xn = xv * r
                if not bwd:
                    y = xn * gain_v
                    if use_rope:
                        y = y * cs + _rope_rot(y, quarter) * sn
                    o_ref[:, cols] = y.astype(o_ref.dtype)
                else:
                    d = dy_ref[:, cols].astype(F32)
                    if use_rope:
                        d = d * cs - _rope_rot(d * sn, quarter)
                    dg_ref[...] += jnp.sum(d * xn, axis=0, keepdims=True)
                    dxn = d * gain_v
                    dx_ref[:, cols] = (r * (dxn - xn * _group_mean(dxn * xn, group))).astype(dx_ref.dtype)

    row = pl.BlockSpec((tb, W), lambda i: (i, 0))
    gw = W if whole else LANES
    gspec = pl.BlockSpec((1, gw), lambda i: (0, 0))
    in_specs, args = [row, gspec], [x, gain]
    if bwd:
        in_specs.append(row)
        args.append(dy)
        if use_mod:
            in_specs.append(row)
            args.append(dres)
    if use_rope:
        tab = pl.BlockSpec((tb, LANES), lambda i: (i, 0))
        in_specs += [tab, tab]
        args += [cos, sin]
    if use_mod:
        two = pl.BlockSpec((2, W), lambda i: (0, 0))
        in_specs += [two, two, pl.BlockSpec((tb, 1), lambda i: (i, 0))]
        args += [shift, scale, mask]
    if not bwd and use_mod:
        out_shape = (jax.ShapeDtypeStruct((T, W), out_dtype), jax.ShapeDtypeStruct((W, T), out_dtype))
        out_specs = (row, pl.BlockSpec((W, tb), lambda i: (0, i)))
    elif not bwd:
        out_shape, out_specs = jax.ShapeDtypeStruct((T, W), out_dtype), row
    else:
        out_shape = [jax.ShapeDtypeStruct((T, W), x.dtype), jax.ShapeDtypeStruct((1, gw), F32)]
        out_specs = [row, gspec]
        if use_mod:
            out_shape += [jax.ShapeDtypeStruct((2, W), F32)] * 2
            out_specs += [pl.BlockSpec((2, W), lambda i: (0, 0))] * 2
    return pl.pallas_call(
        body, name=name, grid=(T // tb,), in_specs=in_specs, out_specs=out_specs, out_shape=out_shape,
        compiler_params=_params("arbitrary" if bwd else "parallel"),
    )(*args)


def make_norm(group, quarter=0, name="norm"):
    kw = dict(group=group, quarter=quarter, shift=None, scale=None, mask=None, out_dtype=BF16)

    @jax.custom_vjp
    def op(x, gain, cos, sin):
        return _norm_call(x, gain, None, cos=cos, sin=sin, name=name + "_fwd", **kw)

    def fwd(x, gain, cos, sin):
        return op(x, gain, cos, sin), (x, gain, cos, sin)

    def bwd(res, dy):
        x, gain, cos, sin = res
        dx, dg = _norm_call(x, gain, dy, cos=cos, sin=sin, name=name + "_bwd", **kw)
        return dx, dg, None, None

    op.defvjp(fwd, bwd)
    return op


@jax.custom_vjp
def modnorm(x, gain, shift, scale, mask):
    h, hT = _norm_call(x, gain, None, group=x.shape[1], quarter=0, cos=None, sin=None, shift=shift, scale=scale,
                       mask=mask, out_dtype=BF16, name="modnorm_fwd")
    return h, hT, x


def _modnorm_fwd(x, gain, shift, scale, mask):
    return modnorm(x, gain, shift, scale, mask), (x, gain, shift, scale, mask)


def _modnorm_bwd(res, cts):
    x, gain, shift, scale, mask = res
    dy, _, dres = cts
    dx, dg, dsh, dsc = _norm_call(x, gain, dy, group=x.shape[1], quarter=0, cos=None, sin=None, shift=shift,
                                  scale=scale, mask=mask, out_dtype=BF16, name="modnorm_bwd", dres=dres)
    return dx, dg, dsh, dsc, None


modnorm.defvjp(_modnorm_fwd, _modnorm_bwd)


def _conv_call(ug, uv, cw, cb, da, *, seq, name):
    T, F = ug.shape
    bwd = da is not None
    tb = _pick(math.gcd(seq, T - seq), ROW_TILES)
    tf = _pick(F, (512, 256, 128))
    nlat, ntot, hpb = seq // tb, T // tb, tb // HALO
    n_ext = tb + 2 * HALO

    def body(*refs):
        if bwd:
            (g_ref, gp_ref, gn_ref, v_ref, vp_ref, vn_ref, d_ref, dp_ref, dn_ref, w_ref, b_ref,
             dg_ref, dv_ref, dw_ref, db_ref) = refs
        else:
            g_ref, gp_ref, gn_ref, v_ref, w_ref, b_ref, o_ref, oT_ref = refs
        i = pl.program_id(1)
        first = jnp.logical_or(i == 0, i == nlat)
        last = jnp.logical_or(i == nlat - 1, i == ntot - 1)
        w = w_ref[...]
        w0, w1, w2, b = w[0:1, :], w[1:2, :], w[2:3, :], b_ref[...]
        if not bwd:
            g = g_ref[...].astype(F32)
            hrow = lax.broadcasted_iota(jnp.int32, (HALO, tf), 0)
            prev = jnp.sum(jnp.where(hrow == HALO - 1, gp_ref[...].astype(F32), 0.0), axis=0, keepdims=True)
            nxt = jnp.sum(jnp.where(hrow == 0, gn_ref[...].astype(F32), 0.0), axis=0, keepdims=True)
            prev = jnp.where(first, 0.0, prev)
            nxt = jnp.where(last, 0.0, nxt)
            row = lax.broadcasted_iota(jnp.int32, g.shape, 0)
            gm1 = jnp.where(row == 0, prev, pltpu.roll(g, 1, 0))
            gp1 = jnp.where(row == tb - 1, nxt, pltpu.roll(g, tb - 1, 0))
            pre = w0 * gm1 + w1 * g + w2 * gp1 + b
            act = pre * jax.nn.sigmoid(pre) * v_ref[...].astype(F32)
            o_ref[...] = act.astype(o_ref.dtype)
            oT_ref[...] = act.T.astype(oT_ref.dtype)
        else:
            @pl.when(i == 0)
            def _():
                dw_ref[...] = jnp.zeros_like(dw_ref)
                db_ref[...] = jnp.zeros_like(db_ref)

            def ext(p_ref, m_ref, n_ref):
                return jnp.concatenate([p_ref[...].astype(F32), m_ref[...].astype(F32), n_ref[...].astype(F32)], axis=0)

            row = lax.broadcasted_iota(jnp.int32, (n_ext, tf), 0)
            valid = jnp.logical_and(jnp.logical_or(row >= HALO, jnp.logical_not(first)),
                                    jnp.logical_or(row < HALO + tb, jnp.logical_not(last)))
            ge = jnp.where(valid, ext(gp_ref, g_ref, gn_ref), 0.0)
            ve = ext(vp_ref, v_ref, vn_ref)
            de = ext(dp_ref, d_ref, dn_ref)
            gm1 = pltpu.roll(ge, 1, 0)
            gp1 = pltpu.roll(ge, n_ext - 1, 0)
            pre = w0 * gm1 + w1 * ge + w2 * gp1 + b
            sg = jax.nn.sigmoid(pre)
            dpre = jnp.where(valid, de * ve * (sg * (1.0 + pre * (1.0 - sg))), 0.0)
            dge = w0 * pltpu.roll(dpre, n_ext - 1, 0) + w1 * dpre + w2 * pltpu.roll(dpre, 1, 0)
            mid = slice(HALO, HALO + tb)
            dg_ref[...] = dge[mid].astype(dg_ref.dtype)
            dv_ref[...] = (de * pre * sg)[mid].astype(dv_ref.dtype)
            dw_ref[0:1, :] += jnp.sum((dpre * gm1)[mid], axis=0, keepdims=True)
            dw_ref[1:2, :] += jnp.sum((dpre * ge)[mid], axis=0, keepdims=True)
            dw_ref[2:3, :] += jnp.sum((dpre * gp1)[mid], axis=0, keepdims=True)
            db_ref[...] += jnp.sum(dpre[mid], axis=0, keepdims=True)

    main = pl.BlockSpec((tb, tf), lambda j, i: (i, j))
    prev = pl.BlockSpec((HALO, tf), lambda j, i: (jnp.maximum(i * hpb - 1, 0), j))
    nxt = pl.BlockSpec((HALO, tf), lambda j, i: (jnp.minimum((i + 1) * hpb, T // HALO - 1), j))
    wspec = pl.BlockSpec((3, tf), lambda j, i: (0, j))
    bspec = pl.BlockSpec((1, tf), lambda j, i: (0, j))
    if not bwd:
        in_specs, args = [main, prev, nxt, main, wspec, bspec], [ug, ug, ug, uv, cw, cb]
        out_shape = (jax.ShapeDtypeStruct((T, F), BF16), jax.ShapeDtypeStruct((F, T), BF16))
        out_specs = (main, pl.BlockSpec((tf, tb), lambda j, i: (j, i)))
    else:
        in_specs = [main, prev, nxt] * 3 + [wspec, bspec]
        args = [ug, ug, ug, uv, uv, uv, da, da, da, cw, cb]
        out_shape = (jax.ShapeDtypeStruct((T, F), BF16), jax.ShapeDtypeStruct((T, F), BF16),
                     jax.ShapeDtypeStruct((3, F), F32), jax.ShapeDtypeStruct((1, F), F32))
        out_specs = (main, main, wspec, bspec)
    return pl.pallas_call(
        body, name=name, grid=(F // tf, T // tb), in_specs=in_specs, out_specs=out_specs, out_shape=out_shape,
        compiler_params=_params("parallel", "arbitrary" if bwd else "parallel"),
    )(*args)


def make_convgate(seq):
    @jax.custom_vjp
    def op(ug, uv, cw, cb):
        return _conv_call(ug, uv, cw, cb, None, seq=seq, name="conv_fwd")

    def fwd(ug, uv, cw, cb):
        return op(ug, uv, cw, cb), (ug, uv, cw, cb)

    def bwd(res, cts):
        ug, uv, cw, cb = res
        return _conv_call(ug, uv, cw, cb, cts[0], seq=seq, name="conv_bwd")

    op.defvjp(fwd, bwd)
    return op


def _attn_tiles(q_len, k_len, bwd):
    if bwd:
        return _pick(q_len, (1024, 512, 256, 128)), _pick(k_len, (768, 512, 256, 128))
    return _pick(q_len, (512, 256, 128)), _pick(k_len, (1408, 768, 512, 256, 128))


def _rows_spec(hm, t, d, head, blk):
    if hm:
        return pl.BlockSpec((None, t, d), lambda h, g, i, j: (head(h, g), blk(i, j), 0))
    return pl.BlockSpec((t, d), lambda h, g, i, j: (blk(i, j), head(h, g)))


def _cols_spec(hm, t, d, head, blk):
    if hm:
        return pl.BlockSpec((None, d, t), lambda h, g, i, j: (head(h, g), 0, blk(i, j)))
    return pl.BlockSpec((d, t), lambda h, g, i, j: (head(h, g), blk(i, j)))


LOG2E = 1.4426950408889634


def _attn_fwd_call(qT, k, vT, prev, *, hm, Hk, G, d, dv, scale, q0, q_len, k0, k_len, total, name):
    tq, tk = _attn_tiles(q_len, k_len, False)
    nq, nk, H = q_len // tq, k_len // tk, Hk * G
    qoff, koff = q0 // tq, k0 // tk
    c2 = scale * LOG2E

    def body(*refs):
        qT_ref, k_ref, vT_ref = refs[:3]
        oT_ref, o_ref, lse_ref, m_s, l_s, acc_s = refs[-6:]
        kj = pl.program_id(3)

        @pl.when(kj == 0)
        def _():
            m_s[...] = jnp.full_like(m_s, -jnp.inf)
            l_s[...] = jnp.zeros_like(l_s)
            acc_s[...] = jnp.zeros_like(acc_s)

        sT = jnp.dot(k_ref[...], qT_ref[...], preferred_element_type=F32)
        m_prev = m_s[...]
        m_new = jnp.maximum(m_prev, jnp.max(sT, axis=0, keepdims=True))
        alpha = jnp.exp2((m_prev - m_new) * c2)
        pT = jnp.exp2((sT - m_new) * c2)
        l_s[...] = alpha * l_s[...] + jnp.sum(pT, axis=0, keepdims=True)
        acc_s[...] = alpha * acc_s[...] + jnp.dot(vT_ref[...], pT.astype(BF16), preferred_element_type=F32)
        m_s[...] = m_new

        @pl.when(kj == nk - 1)
        def _():
            res = acc_s[...] / l_s[...]
            oT_ref[...] = res.astype(oT_ref.dtype)
            o_ref[...] = res.T.astype(o_ref.dtype)
            lse_ref[...] = m_s[...] * scale + jnp.log(l_s[...])

    qh, kh = (lambda h, g: h * G + g), (lambda h, g: h)
    qb, kb = (lambda i, j: qoff + i), (lambda i, j: koff + j)
    in_specs = [_cols_spec(hm, tq, d, qh, qb), _rows_spec(hm, tk, d, kh, kb), _cols_spec(False, tk, dv, kh, kb)]
    args, alias = [qT, k, vT], {}
    if prev is not None:
        in_specs += [HBM_SPEC, HBM_SPEC, HBM_SPEC]
        args += list(prev)
        alias = {3: 0, 4: 1, 5: 2}
    return pl.pallas_call(
        body, name=name, grid=(Hk, G, nq, nk), in_specs=in_specs,
        out_specs=(_cols_spec(False, tq, dv, qh, qb), _rows_spec(False, tq, dv, qh, qb),
                   pl.BlockSpec((None, 1, tq), lambda h, g, i, j: (h * G + g, 0, qoff + i))),
        out_shape=(jax.ShapeDtypeStruct((H * dv, total), BF16), jax.ShapeDtypeStruct((total, H * dv), BF16),
                   jax.ShapeDtypeStruct((H, 1, total), F32)),
        scratch_shapes=[pltpu.VMEM((1, tq), F32), pltpu.VMEM((1, tq), F32), pltpu.VMEM((dv, tq), F32)],
        input_output_aliases=alias,
        compiler_params=_params("parallel", "parallel", "parallel", "arbitrary"),
    )(*args)


def _attn_bwd_call(qT, q, k, kT, v, doT, do, oT, lse, prev, *, hm, Hk, G, d, dv, scale, q0, q_len, k0, k_len, total, name):
    tq, tk = _attn_tiles(q_len, k_len, True)
    nq, nk, H = q_len // tq, k_len // tk, Hk * G
    qoff, koff = q0 // tq, k0 // tk
    c2 = scale * LOG2E
    acc_in = prev is not None

    def body(*refs):
        qT_ref, q_ref, k_ref, kT_ref, v_ref, doT_ref, do_ref, oT_ref, lse_ref = refs[:9]
        dqT_ref, dk_ref, dv_ref, dq_s, delta_s = refs[-5:]
        g, qi, kj = pl.program_id(1), pl.program_id(2), pl.program_id(3)

        @pl.when(jnp.logical_and(jnp.logical_and(g == 0, qi == 0), kj == 0))
        def _():
            if acc_in:
                dk_ref[...] = refs[10][...]
                dv_ref[...] = refs[11][...]
            else:
                dk_ref[...] = jnp.zeros_like(dk_ref)
                dv_ref[...] = jnp.zeros_like(dv_ref)

        doT_v = doT_ref[...]

        @pl.when(kj == 0)
        def _():
            delta_s[...] = jnp.sum(doT_v.astype(F32) * oT_ref[...].astype(F32), axis=0, keepdims=True)
            dq_s[...] = jnp.zeros_like(dq_s)

        sT = jnp.dot(k_ref[...], qT_ref[...], preferred_element_type=F32)
        pT = jnp.exp2(sT * c2 - lse_ref[...] * LOG2E)
        rows = pl.ds(pl.multiple_of(kj * tk, tk), tk)
        dv_ref[rows, :] += jnp.dot(pT.astype(BF16), do_ref[...], preferred_element_type=F32)
        dpT = jnp.dot(v_ref[...], doT_v, preferred_element_type=F32)
        dsT = (pT * (dpT - delta_s[...]) * scale).astype(BF16)
        dk_ref[rows, :] += jnp.dot(dsT, q_ref[...], preferred_element_type=F32)
        dq_s[...] += jnp.dot(kT_ref[...], dsT, preferred_element_type=F32)

        @pl.when(kj == nk - 1)
        def _():
            dqT_ref[...] = dq_s[...].astype(dqT_ref.dtype)

    qh, kh = (lambda h, g: h * G + g), (lambda h, g: h)
    qb, kb = (lambda i, j: qoff + i), (lambda i, j: koff + j)
    kres = lambda i, j: k0 // k_len
    in_specs = [_cols_spec(hm, tq, d, qh, qb), _rows_spec(hm, tq, d, qh, qb), _rows_spec(hm, tk, d, kh, kb),
                _cols_spec(hm, tk, d, kh, kb), _rows_spec(False, tk, dv, kh, kb), _cols_spec(False, tq, dv, qh, qb),
                _rows_spec(False, tq, dv, qh, qb), _cols_spec(False, tq, dv, qh, qb),
                pl.BlockSpec((None, 1, tq), lambda h, g, i, j: (h * G + g, 0, qoff + i))]
    args, alias = [qT, q, k, kT, v, doT, do, oT, lse], {}
    dk_spec, dv_spec = _rows_spec(hm, k_len, d, kh, kres), _rows_spec(False, k_len, dv, kh, kres)
    if acc_in:
        in_specs += [HBM_SPEC, dk_spec, dv_spec]
        args += list(prev)
        alias = {9: 0, 10: 1, 11: 2}
    dq_shape = (H, d, total) if hm else (H * d, total)
    dk_shape = (Hk, total, d) if hm else (total, Hk * d)
    return pl.pallas_call(
        body, name=name, grid=(Hk, G, nq, nk), in_specs=in_specs,
        out_specs=(_cols_spec(hm, tq, d, qh, qb), dk_spec, dv_spec),
        out_shape=(jax.ShapeDtypeStruct(dq_shape, BF16), jax.ShapeDtypeStruct(dk_shape, F32),
                   jax.ShapeDtypeStruct((total, Hk * dv), F32)),
        scratch_shapes=[pltpu.VMEM((d, tq), F32), pltpu.VMEM((1, tq), F32)],
        input_output_aliases=alias,
        compiler_params=_params("parallel", "arbitrary", "arbitrary", "arbitrary"),
    )(*args)


def make_attention(*, hm, Hk, G, d, dv, scale, seq, total, name):
    kw = dict(hm=hm, Hk=Hk, G=G, d=d, dv=dv, scale=scale, total=total)
    lat = dict(q0=0, q_len=seq, k0=0, k_len=total)
    ctx = dict(q0=seq, q_len=total - seq, k0=seq, k_len=total - seq)
    tr = (lambda a: jnp.swapaxes(a, 1, 2)) if hm else (lambda a: a.T)

    def run(q, k, v):
        qT, vT = tr(q), v.T
        first = _attn_fwd_call(qT, k, vT, None, name=name + "_fwd_lat", **kw, **lat)
        return _attn_fwd_call(qT, k, vT, first, name=name + "_fwd_ctx", **kw, **ctx)

    @jax.custom_vjp
    def op(q, k, v):
        oT, o, _ = run(q, k, v)
        return o, oT

    def fwd(q, k, v):
        oT, o, lse = run(q, k, v)
        return (o, oT), (q, k, v, oT, lse)

    def bwd(res, cts):
        q, k, v, oT, lse = res
        do = cts[0]
        ops = (tr(q), q, k, tr(k), v, do.T, do, oT, lse)
        first = _attn_bwd_call(*ops, None, name=name + "_bwd_lat", **kw, **lat)
        dqT, dk, dv_ = _attn_bwd_call(*ops, first, name=name + "_bwd_ctx", **kw, **ctx)
        return tr(dqT).astype(q.dtype), dk.astype(k.dtype), dv_.astype(v.dtype)

    op.defvjp(fwd, bwd)
    return op


def _loss_call(xf, target):
    S, D = target.shape
    tb = _pick(S, ROW_TILES)

    def body(y_ref, t_ref, dy_ref, acc_ref):
        @pl.when(pl.program_id(0) == 0)
        def _():
            acc_ref[...] = jnp.zeros_like(acc_ref)

        e = y_ref[...] - t_ref[...]
        dy_ref[...] = e * (1.0 / D)
        acc_ref[...] += jnp.sum(e * e, axis=0, keepdims=True)

    row = pl.BlockSpec((tb, D), lambda i: (i, 0))
    return pl.pallas_call(
        body, name="loss", grid=(S // tb,), in_specs=[row, row],
        out_specs=(row, pl.BlockSpec((1, D), lambda i: (0, 0))),
        out_shape=(jax.ShapeDtypeStruct((S, D), F32), jax.ShapeDtypeStruct((1, D), F32)),
        compiler_params=_params("arbitrary"),
    )(xf, target)


def _adamw_call(w, g, m, v, name):
    R, N = w.shape
    tb = _pick(R, tuple(t for t in (512, 256, 128, 64, 32, 16, 8) if t * N * 4 <= (1 << 20)) or (8,))
    bc1, bc2 = 1.0 - ADAM_B1 ** ADAM_STEP, 1.0 - ADAM_B2 ** ADAM_STEP

    def body(w_ref, g_ref, m_ref, v_ref, d_ref, nm_ref, nv_ref):
        gv = g_ref[...]
        nm = ADAM_B1 * m_ref[...] + (1.0 - ADAM_B1) * gv
        nv = ADAM_B2 * v_ref[...] + (1.0 - ADAM_B2) * (gv * gv)
        d_ref[...] = -ADAM_LR * ((nm / bc1) / (jnp.sqrt(nv / bc2) + ADAM_EPS) + ADAM_WD * w_ref[...])
        nm_ref[...] = nm
        nv_ref[...] = nv

    spec = pl.BlockSpec((tb, N), lambda i: (i, 0))
    shp = jax.ShapeDtypeStruct((R, N), F32)
    return pl.pallas_call(
        body, name=name, grid=(R // tb,), in_specs=[spec] * 4, out_specs=(spec,) * 3, out_shape=(shp,) * 3,
        compiler_params=_params("parallel"),
    )(w, g, m, v)


def _adamw(w, g, m, v, name="adamw"):
    shape = w.shape
    two = (lambda a: a.reshape(-1, shape[-1])) if w.ndim >= 2 else (lambda a: a.reshape(1, -1))
    return tuple(r.reshape(shape) for r in _adamw_call(two(w), two(g), two(m), two(v), name))


def _me():
    return lax.axis_index("x"), lax.axis_index("y"), lax.axis_index("c")


def _gather8(x, *, reduce, name):
    R, W = x.shape

    def body(x_ref, out_ref, *scratch):
        if reduce:
            buf, send_sems, recv_sems, local_sem = scratch
        else:
            buf = out_ref
            send_sems, recv_sems, local_sem = scratch
        x, y, c = _me()
        me, sibling = (x, y, c), (x, y, 1 - c)
        chips = [(1 - x, y), (x, 1 - y), (1 - x, 1 - y)]

        def rows(px, py, pc):
            return buf.at[pl.ds((4 * px + 2 * py + pc) * R, R), :]

        def copy(k, block, to, src=None):
            return pltpu.make_async_remote_copy(
                src_ref=rows(*block) if src is None else src, dst_ref=rows(*block),
                send_sem=send_sems.at[k], recv_sem=recv_sems.at[k], device_id=to, device_id_type=MESH)

        mine = pltpu.make_async_copy(x_ref, rows(*me), local_sem)
        mine.start()
        first = [copy(0, me, sibling, src=x_ref)]
        first += [copy(1 + j, me, (*chip, c), src=x_ref) for j, chip in enumerate(chips)]
        for cp in first:
            cp.start()
        passed = [copy(4 + j, (*chip, c), sibling) for j, chip in enumerate(chips)]
        for j, chip in enumerate(chips):
            copy(1 + j, (*chip, c), me).wait_recv()
            passed[j].start()
        copy(0, sibling, me).wait_recv()
        for j, chip in enumerate(chips):
            copy(4 + j, (*chip, 1 - c), me).wait_recv()
        for cp in first + passed:
            cp.wait_send()
        mine.wait()
        if reduce:
            acc = buf[0:R, :]
            for e in range(1, 8):
                acc = acc + buf[e * R:(e + 1) * R, :]
            out_ref[...] = acc

    sems = [pltpu.SemaphoreType.DMA((7,)), pltpu.SemaphoreType.DMA((7,)), pltpu.SemaphoreType.DMA]
    return pl.pallas_call(
        body, name=name,
        out_shape=jax.ShapeDtypeStruct((R if reduce else 8 * R, W), F32),
        in_specs=[pl.BlockSpec(memory_space=pltpu.VMEM)], out_specs=pl.BlockSpec(memory_space=pltpu.VMEM),
        scratch_shapes=([pltpu.VMEM((8 * R, W), F32)] if reduce else []) + sems,
        compiler_params=pltpu.CompilerParams(vmem_limit_bytes=VMEM_LIMIT),
    )(x)


def _small_exchange(flat, *, reduce, name):
    n = flat.shape[0]
    W = PACK_W if n >= 8 * PACK_W else LANES
    R = -(-n // (8 * W)) * 8
    x = jnp.pad(flat, (0, R * W - n)).reshape(R, W)
    out = _gather8(x, reduce=reduce, name=name)
    if reduce:
        return out.reshape(-1)[:n]
    return out.reshape(8, R * W)[:, :n]


HBM_SPEC = pl.BlockSpec(memory_space=pl.ANY)


def _chip_gather(mine):
    _, R, W = mine.shape

    def body(m_ref, out_ref, send_sems, recv_sems):
        x, y, c = _me()
        sibling = (x, y, 1 - c)
        chips = [(1 - x, y), (x, 1 - y), (1 - x, 1 - y)]

        def slot(px, py, half):
            return out_ref.at[half, 2 * px + py]

        def copy(k, src, dst, to):
            return pltpu.make_async_remote_copy(src_ref=src, dst_ref=dst, send_sem=send_sems.at[k],
                                                recv_sem=recv_sems.at[k], device_id=to, device_id_type=MESH)

        first = [copy(j, m_ref.at[c], slot(x, y, c), (*chip, c)) for j, chip in enumerate(chips)]
        for cp in first:
            cp.start()
        passed = [copy(3 + j, slot(*chip, c), slot(*chip, c), sibling) for j, chip in enumerate(chips)]
        for j, chip in enumerate(chips):
            copy(j, m_ref.at[c], slot(*chip, c), (*chip, c)).wait_recv()
            passed[j].start()
        for j, chip in enumerate(chips):
            copy(3 + j, slot(*chip, 1 - c), slot(*chip, 1 - c), sibling).wait_recv()
        for cp in first + passed:
            cp.wait_send()

    out = pl.pallas_call(
        body, name="chip_gather", out_shape=jax.ShapeDtypeStruct((2, 4, R, W), mine.dtype),
        in_specs=[HBM_SPEC], out_specs=HBM_SPEC,
        scratch_shapes=[pltpu.SemaphoreType.DMA((6,)), pltpu.SemaphoreType.DMA((6,))],
    )(mine)
    x, y, _ = _me()
    return lax.dynamic_update_slice(out, mine[:, None], (0, 2 * x + y, 0, 0))


def _pair_swap(buf):
    def body(b_ref, out_ref, send_sem, recv_sem):
        x, y, c = _me()
        cp = pltpu.make_async_remote_copy(src_ref=b_ref.at[1 - c], dst_ref=out_ref, send_sem=send_sem,
                                          recv_sem=recv_sem, device_id=(x, y, 1 - c), device_id_type=MESH)
        cp.start()
        cp.wait()

    return pl.pallas_call(
        body, name="pair_swap_grads", out_shape=jax.ShapeDtypeStruct(buf.shape[1:], buf.dtype), in_specs=[HBM_SPEC],
        out_specs=HBM_SPEC, scratch_shapes=[pltpu.SemaphoreType.DMA, pltpu.SemaphoreType.DMA],
    )(buf)


def _pair_share(half):
    def body(h_ref, out_ref, send_sem, recv_sem):
        x, y, c = _me()
        cp = pltpu.make_async_remote_copy(src_ref=h_ref, dst_ref=out_ref.at[c], send_sem=send_sem, recv_sem=recv_sem,
                                          device_id=(x, y, 1 - c), device_id_type=MESH)
        cp.start()
        pltpu.make_async_remote_copy(src_ref=h_ref, dst_ref=out_ref.at[1 - c], send_sem=send_sem, recv_sem=recv_sem,
                                     device_id=(x, y, 1 - c), device_id_type=MESH).wait_recv()
        cp.wait_send()

    out = pl.pallas_call(
        body, name="pair_share", out_shape=jax.ShapeDtypeStruct((2,) + half.shape, half.dtype), in_specs=[HBM_SPEC],
        out_specs=HBM_SPEC, scratch_shapes=[pltpu.SemaphoreType.DMA, pltpu.SemaphoreType.DMA],
    )(half)
    return lax.dynamic_update_slice(out, half[None], (lax.axis_index("c"), 0, 0))


def _chip_exchange(s):
    def body(s_ref, out_ref, send_sems, recv_sems):
        x, y, c = _me()
        q = 2 * x + y
        chips = [(1 - x, y), (x, 1 - y), (1 - x, 1 - y)]

        def copy(j, chip):
            return pltpu.make_async_remote_copy(
                src_ref=s_ref.at[2 * chip[0] + chip[1]], dst_ref=out_ref.at[q], send_sem=send_sems.at[j],
                recv_sem=recv_sems.at[j], device_id=(*chip, c), device_id_type=MESH)

        sends = [copy(j, chip) for j, chip in enumerate(chips)]
        for cp in sends:
            cp.start()
        for j, chip in enumerate(chips):
            pltpu.make_async_remote_copy(
                src_ref=s_ref.at[q], dst_ref=out_ref.at[2 * chip[0] + chip[1]], send_sem=send_sems.at[j],
                recv_sem=recv_sems.at[j], device_id=(*chip, c), device_id_type=MESH).wait_recv()
        for cp in sends:
            cp.wait_send()

    return pl.pallas_call(
        body, name="chip_exchange", out_shape=jax.ShapeDtypeStruct(s.shape, s.dtype), in_specs=[HBM_SPEC],
        out_specs=HBM_SPEC,
        scratch_shapes=[pltpu.SemaphoreType.DMA((3,)), pltpu.SemaphoreType.DMA((3,))],
    )(s)


def _pair_sum(p, got, c):
    _, _, R, W = p.shape
    tb = _pick(R, ROW_TILES)

    def body(c_ref, p_ref, g_ref, o_ref):
        o_ref[...] = (p_ref[...].astype(F32) + g_ref[...].astype(F32)).astype(o_ref.dtype)

    return pl.pallas_call(
        body, name="pair_sum",
        grid_spec=pltpu.PrefetchScalarGridSpec(
            num_scalar_prefetch=1, grid=(4, R // tb),
            in_specs=[pl.BlockSpec((None, None, tb, W), lambda s, i, cr: (cr[0], s, i, 0)),
                      pl.BlockSpec((None, tb, W), lambda s, i, cr: (s, i, 0))],
            out_specs=pl.BlockSpec((None, tb, W), lambda s, i, cr: (s, i, 0))),
        out_shape=jax.ShapeDtypeStruct((4, R, W), BF16),
        compiler_params=_params("parallel", "parallel"),
    )(jnp.reshape(c, (1,)).astype(jnp.int32), p, got)


def _sum4(r, s, q):
    _, R, W = r.shape
    tb = _pick(R, ROW_TILES)

    def body(q_ref, r_ref, s_ref, o_ref):
        own = s_ref[...].astype(F32)
        acc = jnp.where(q_ref[0] == 0, own, r_ref[0].astype(F32))
        for a in range(1, 4):
            acc = acc + jnp.where(q_ref[0] == a, own, r_ref[a].astype(F32))
        o_ref[...] = acc

    return pl.pallas_call(
        body, name="sum4",
        grid_spec=pltpu.PrefetchScalarGridSpec(
            num_scalar_prefetch=1, grid=(R // tb,),
            in_specs=[pl.BlockSpec((4, tb, W), lambda i, qr: (0, i, 0)),
                      pl.BlockSpec((None, tb, W), lambda i, qr: (qr[0], i, 0))],
            out_specs=pl.BlockSpec((tb, W), lambda i, qr: (i, 0))),
        out_shape=jax.ShapeDtypeStruct((R, W), F32),
        compiler_params=_params("parallel"),
    )(jnp.reshape(q, (1,)).astype(jnp.int32), r, s)


BIG = (("mla_w_dq", 1), ("mla_w_uq", 2), ("mla_w_dkv", 1), ("mla_w_ukv", 2), ("mla_w_o", 1),
       ("gqa_w_q", 1), ("gqa_w_kv", 1), ("gqa_w_o", 1), ("ffn_w_up", 2), ("ffn_w_down", 1))


def _pack_rows(n):
    return -(-n // (2 * PACK_W * ROW_TILES[0])) * ROW_TILES[0]


def _to_shards(full, axis):
    L, K, N = full.shape
    if axis == 1:
        return full.reshape(L, 4, K // 4, N).transpose(1, 0, 2, 3).reshape(4, -1)
    return full.reshape(L, K, 4, N // 4).transpose(2, 0, 1, 3).reshape(4, -1)


def _from_shards(flat4, shard_shape, axis):
    L, K, N = shard_shape
    a = flat4.reshape(4, L, K, N)
    if axis == 1:
        return a.transpose(1, 0, 2, 3).reshape(L, 4 * K, N)
    return a.transpose(1, 2, 0, 3).reshape(L, K, 4 * N)


def _rope_tables(seq, total, rot_dim):
    t = jnp.arange(seq, dtype=jnp.int32)
    rows, cols = t // GRID_W, t % GRID_W
    axis_dim = rot_dim // 2
    inv = jnp.power(ROPE_BASE, -jnp.arange(0, axis_dim, 2, dtype=F32) / axis_dim)
    ang_r = rows.astype(F32)[:, None] * inv
    ang_c = cols.astype(F32)[:, None] * inv
    ang = jnp.concatenate([ang_r, ang_r, ang_c, ang_c], axis=-1)
    cos = jnp.concatenate([jnp.cos(ang), jnp.ones((total - seq, rot_dim), F32)], axis=0)
    sin = jnp.concatenate([jnp.sin(ang), jnp.zeros((total - seq, rot_dim), F32)], axis=0)
    rep = LANES // rot_dim
    return jnp.tile(cos, (1, rep)), jnp.tile(sin, (1, rep))


def _forward(X, mod, P, *, seq):
    T, D = X.shape
    depth = mod.shape[0]
    mask = (jnp.arange(T) >= seq).astype(F32)[:, None]
    rope_m = _rope_tables(seq, T, ROPE_D)
    rope_g = _rope_tables(seq, T, HEAD)
    Hm = P["mla_w_uq"].shape[-1] // (NOPE + ROPE_D)
    Hg = P["gqa_w_q"].shape[-1] // HEAD
    Hkv = P["gqa_w_kv"].shape[-1] // (2 * HEAD)
    kv_rank = P["mla_w_dkv"].shape[-1] - ROPE_D
    dff = P["ffn_conv_b"].shape[-1]

    norm_row = make_norm(P["mla_w_dq"].shape[-1], name="norm_rank")
    norm_kv = make_norm(kv_rank, name="norm_kvrank")
    norm_head = make_norm(LANES, name="norm_head")
    norm_pe = make_norm(ROPE_D, ROPE_D // 4, name="norm_pe")
    norm_gqa = make_norm(HEAD, HEAD // 4, name="norm_gqa")
    attn_mla = make_attention(hm=True, Hk=Hm, G=1, d=NOPE + ROPE_D, dv=VDIM, scale=1.0 / math.sqrt(NOPE + ROPE_D),
                              seq=seq, total=T, name="mla")
    attn_gqa = make_attention(hm=False, Hk=Hkv, G=Hg // Hkv, d=HEAD, dv=HEAD, scale=1.0 / math.sqrt(HEAD),
                              seq=seq, total=T, name="gqa")
    convgate = make_convgate(seq)
    row = lambda g: g.reshape(1, -1)
    twice = lambda g: jnp.concatenate([g, g]).reshape(1, -1)

    for i in range(depth):
        j = i // 2
        sh1, sc1, g1, sh2, sc2, g2 = [mod[i][:, k * D:(k + 1) * D] for k in range(N_MOD)]
        h, hT, X = modnorm(X, row(P["norm_mix"][i]), sh1, sc1, mask)
        if i % 2 == 0:
            w_uq = P["mla_w_uq"][j].reshape(-1, Hm, NOPE + ROPE_D)
            w_ukv = P["mla_w_ukv"][j].reshape(-1, Hm, NOPE + VDIM)
            cq = norm_row(linear(h, hT, P["mla_w_dq"][j]), row(P["mla_g_dq"][j]), None, None)
            cqT = cq.T
            qn = norm_head(linear(cq, cqT, w_uq[:, :, :NOPE].reshape(-1, Hm * NOPE)), row(P["mla_g_q_nope"][j]), None, None)
            qp = norm_pe(linear(cq, cqT, w_uq[:, :, NOPE:].reshape(-1, Hm * ROPE_D)), twice(P["mla_g_q_pe"][j]), *rope_m)
            kva = linear(h, hT, P["mla_w_dkv"][j])
            ckv = norm_kv(kva[:, :kv_rank], row(P["mla_g_dkv"][j]), None, None)
            ckvT = ckv.T
            kp_raw = kva[:, kv_rank:]
            kp = norm_pe(jnp.concatenate([kp_raw, kp_raw], axis=-1), twice(P["mla_g_k_pe"][j]), *rope_m)[:, :ROPE_D]
            kn = norm_head(linear(ckv, ckvT, w_ukv[:, :, :NOPE].reshape(-1, Hm * NOPE)), row(P["mla_g_k_nope"][j]), None, None)
            v = linear_bf16(ckv, ckvT, w_ukv[:, :, NOPE:].reshape(-1, Hm * VDIM))
            q_hm = jnp.concatenate([qn.reshape(T, Hm, NOPE), qp.reshape(T, Hm, ROPE_D)], axis=-1).transpose(1, 0, 2)
            k_hm = jnp.concatenate([kn.reshape(T, Hm, NOPE), jnp.broadcast_to(kp[:, None, :], (T, Hm, ROPE_D))],
                                   axis=-1).transpose(1, 0, 2)
            o, oT = attn_mla(q_hm, k_hm, v)
            X = linear_gated(o, oT, P["mla_w_o"][j], X, g1, mask)
        else:
            w_kv = P["gqa_w_kv"][j]
            q = norm_gqa(linear(h, hT, P["gqa_w_q"][j]), row(P["gqa_g_q"][j]), *rope_g)
            k = norm_gqa(linear(h, hT, w_kv[:, :Hkv * HEAD]), row(P["gqa_g_k"][j]), *rope_g)
            v = linear_bf16(h, hT, w_kv[:, Hkv * HEAD:])
            o, oT = attn_gqa(q, k, v)
            X = linear_gated(o, oT, P["gqa_w_o"][j], X, g1, mask)
        h2, h2T, X = modnorm(X, row(P["norm_ffn"][i]), sh2, sc2, mask)
        w_up = P["ffn_w_up"][i]
        ug = linear_bf16(h2, h2T, w_up[:, :dff])
        uv = linear_bf16(h2, h2T, w_up[:, dff:])
        a, aT = convgate(ug, uv, P["ffn_conv_w"][i], row(P["ffn_conv_b"][i]))
        X = linear_gated(a, aT, P["ffn_w_down"][i], X, g2, mask)
    return X


SMALL = ("norm_mix", "norm_ffn", "mla_g_dq", "mla_g_q_nope", "mla_g_q_pe", "mla_g_dkv", "mla_g_k_pe",
         "mla_g_k_nope", "gqa_g_q", "gqa_g_k", "ffn_conv_w", "ffn_conv_b")
WEIGHTS = ("c_ctx", "w_mod", "b_mod", "norm_mix", "norm_ffn", "mla_w_dq", "mla_g_dq", "mla_w_uq", "mla_g_q_nope",
           "mla_g_q_pe", "mla_w_dkv", "mla_g_dkv", "mla_g_k_pe", "mla_w_ukv", "mla_g_k_nope", "mla_w_o", "gqa_w_q",
           "gqa_g_q", "gqa_w_kv", "gqa_g_k", "gqa_w_o", "ffn_w_up", "ffn_conv_w", "ffn_conv_b", "ffn_w_down")


def _step(A):
    x, y, c = _me()
    q = 2 * x + y
    dev = 4 * x + 2 * y + c
    seq, D = A["x"].shape[1], A["x"].shape[2]
    depth = A["w_mod"].shape[0]
    X0 = jnp.concatenate([A["x"][0], A["ctx"][0]], axis=0)

    shard_n = [math.prod(A[n].shape) for n, _ in BIG]
    n_tot = sum(shard_n)
    R = _pack_rows(n_tot)
    flat = jnp.concatenate([A[n].astype(BF16).reshape(-1) for n, _ in BIG])
    mine = jnp.pad(flat, (0, 2 * R * PACK_W - n_tot)).reshape(2, R, PACK_W)
    allw = _chip_gather(mine).transpose(1, 0, 2, 3).reshape(4, -1)
    P, off = {}, 0
    for (n, axis), cnt in zip(BIG, shard_n):
        P[n] = _from_shards(allw[:, off:off + cnt], A[n].shape, axis)
        off += cnt

    c_all = _small_exchange(A["c"].reshape(-1), reduce=False, name="gather_c")
    cw = _small_exchange(A["ffn_conv_w"].reshape(-1), reduce=False, name="gather_convw")
    cw = cw.reshape(4, 2, depth, 3, -1)[:, 0].transpose(1, 2, 0, 3).reshape(depth, 3, -1)
    cvec = jnp.concatenate([c_all, jnp.broadcast_to(A["c_ctx"][None, :], (8, D))], axis=0)
    act = jax.nn.silu(cvec)
    ncol = A["w_mod"].shape[-1]
    modcols = jnp.stack([_mm(act, A["w_mod"][i], name="mod_fwd") for i in range(depth)])
    mod_all = _small_exchange(modcols.reshape(-1), reduce=False, name="gather_mod")
    mod_all = mod_all.reshape(4, 2, depth, 16, ncol)[:, 0].transpose(1, 2, 0, 3).reshape(depth, 16, 4 * ncol)
    mod_all = mod_all + A["b_mod"][:, None, :]
    mod = jnp.stack([lax.dynamic_index_in_dim(mod_all, dev, 1, keepdims=False), mod_all[:, 8]], axis=1)

    for n in SMALL:
        P[n] = cw if n == "ffn_conv_w" else A[n]

    XF, vjp = jax.vjp(functools.partial(_forward, seq=seq), X0, mod, P)
    dy, sq = _loss_call(XF, A["loss_target"][0])
    loss_part = (0.5 / D * jnp.sum(sq)).reshape(1)
    dX0, dmod, dP = vjp(jnp.concatenate([dy, jnp.zeros((X0.shape[0] - seq, D), F32)], axis=0))
    grad_x = dX0[:seq][None]

    G = {}
    dmod_all = _small_exchange(dmod.reshape(-1), reduce=False, name="gather_dmod").reshape(8, depth, 2, -1)
    drows = jnp.concatenate([dmod_all[:, :, 0], dmod_all[:, :, 1]], axis=0).transpose(1, 0, 2)
    G["b_mod"] = jnp.sum(drows, axis=1)
    dcols = lax.dynamic_slice_in_dim(drows, q * ncol, ncol, axis=2)
    G["w_mod"] = jnp.stack([_mm(act, dcols[i], ta=True, name="mod_dw") for i in range(depth)])
    dact = sum(_mm(dcols[i], A["w_mod"][i], tb=True, name="mod_dact") for i in range(depth))
    dcc_part = 0.5 * jnp.sum(dact[8:], axis=0)

    small_list = [dP[n].astype(F32).reshape(-1) for n in SMALL] + [dcc_part]
    sizes = [a.shape[0] for a in small_list]
    tot = _small_exchange(jnp.concatenate(small_list + [loss_part]), reduce=True, name="reduce_small")
    loss = tot[-1]
    off = 0
    for n, cnt in zip(SMALL + ("c_ctx",), sizes):
        G[n] = tot[off:off + cnt].reshape(P[n].shape if n != "c_ctx" else (D,))
        off += cnt
    sg = jax.nn.sigmoid(A["c_ctx"])
    G["c_ctx"] = G["c_ctx"] * (sg * (1.0 + A["c_ctx"] * (1.0 - sg)))
    fcols = A["ffn_conv_w"].shape[-1]
    G["ffn_conv_w"] = lax.dynamic_slice_in_dim(G["ffn_conv_w"], q * fcols, fcols, axis=2)

    gflat = jnp.concatenate([_to_shards(dP[n], axis) for n, axis in BIG], axis=1)
    gp = jnp.pad(gflat, ((0, 0), (0, 2 * R * PACK_W - n_tot))).reshape(4, 2, R, PACK_W).transpose(1, 0, 2, 3)
    got = _pair_swap(gp)
    pair = _pair_sum(gp, got, c)
    half = _sum4(_chip_exchange(pair), pair, q)
    both = _pair_share(half).reshape(-1)
    off = 0
    for (n, _), cnt in zip(BIG, shard_n):
        G[n] = both[off:off + cnt].reshape(A[n].shape)
        off += cnt

    delta, new_m, new_v = {}, {}, {}
    tiny = [n for n in WEIGHTS if A[n].size < (1 << 18)]
    cat = lambda pre: jnp.concatenate([(A[pre + n] if pre != "g" else G[n]).reshape(-1) for n in tiny])
    n_tiny = sum(A[n].size for n in tiny)
    rt = -(-n_tiny // (8 * PACK_W)) * 8
    shape2 = lambda a: jnp.pad(a, (0, rt * PACK_W - n_tiny)).reshape(rt, PACK_W)
    outs = _adamw_call(shape2(cat("")), shape2(cat("g")), shape2(cat("m_")), shape2(cat("v_")), "adamw_small")
    off = 0
    for n in tiny:
        cnt = A[n].size
        delta[n], new_m[n], new_v[n] = [o.reshape(-1)[off:off + cnt].reshape(A[n].shape) for o in outs]
        off += cnt
    for n in WEIGHTS:
        if n not in tiny:
            delta[n], new_m[n], new_v[n] = _adamw(A[n], G[n], A["m_" + n], A["v_" + n])
    return (loss, grad_x, *[G[n] for n in WEIGHTS], *[delta[n] for n in WEIGHTS],
            *[new_m[n] for n in WEIGHTS], *[new_v[n] for n in WEIGHTS])


def kernel(x, c, ctx, c_ctx, w_mod, b_mod, norm_mix, norm_ffn, mla_w_dq, mla_g_dq, mla_w_uq, mla_g_q_nope, mla_g_q_pe, mla_w_dkv, mla_g_dkv, mla_g_k_pe, mla_w_ukv, mla_g_k_nope, mla_w_o, gqa_w_q, gqa_g_q, gqa_w_kv, gqa_g_k, gqa_w_o, ffn_w_up, ffn_conv_w, ffn_conv_b, ffn_w_down, loss_target, m_c_ctx, m_w_mod, m_b_mod, m_norm_mix, m_norm_ffn, m_mla_w_dq, m_mla_g_dq, m_mla_w_uq, m_mla_g_q_nope, m_mla_g_q_pe, m_mla_w_dkv, m_mla_g_dkv, m_mla_g_k_pe, m_mla_w_ukv, m_mla_g_k_nope, m_mla_w_o, m_gqa_w_q, m_gqa_g_q, m_gqa_w_kv, m_gqa_g_k, m_gqa_w_o, m_ffn_w_up, m_ffn_conv_w, m_ffn_conv_b, m_ffn_w_down, v_c_ctx, v_w_mod, v_b_mod, v_norm_mix, v_norm_ffn, v_mla_w_dq, v_mla_g_dq, v_mla_w_uq, v_mla_g_q_nope, v_mla_g_q_pe, v_mla_w_dkv, v_mla_g_dkv, v_mla_g_k_pe, v_mla_w_ukv, v_mla_g_k_nope, v_mla_w_o, v_gqa_w_q, v_gqa_g_q, v_gqa_w_kv, v_gqa_g_k, v_gqa_w_o, v_ffn_w_up, v_ffn_conv_w, v_ffn_conv_b, v_ffn_w_down):
    return _step(dict(locals()))
```

```python
import functools
import math

import jax
import jax.numpy as jnp
from jax import lax
from jax.experimental import pallas as pl
from jax.experimental.pallas import tpu as pltpu

F32, BF16 = jnp.float32, jnp.bfloat16
MESH = pl.DeviceIdType.MESH

EPS = 1e-6
ROPE_BASE = 10000.0
GRID_W = 64
NOPE, ROPE_D, VDIM = 128, 64, 128
HEAD = 128
N_MOD = 6
LANES = 128
HALO = 16
VMEM_LIMIT = 56 * 1024 * 1024
PACK_W = 1024

ADAM_LR, ADAM_B1, ADAM_B2, ADAM_EPS, ADAM_WD, ADAM_STEP = 0.001, 0.9, 0.999, 1e-08, 0.01, 10

LANE_TILES = (1024, 768, 512, 256, 128)
K_RESIDENT = 2048
MM_TILES = (2048, 1408, 1024, 768, 512, 384, 256, 128)
MM_VMEM_BUDGET = 40 * 1024 * 1024
STEP_COST_BYTES = 1 << 20
ACC_DISCOUNT = 4
ROW_TILES = (256, 128, 64, 32, 16, 8)


def _pick(n, prefs):
    for p in prefs:
        if n % p == 0:
            return p
    return n


def _params(*sem):
    return pltpu.CompilerParams(dimension_semantics=sem, vmem_limit_bytes=VMEM_LIMIT)


def _mm_tiles(M, N, K, sa, sb, so, fused):
    def options(n, cap):
        opts = [t for t in MM_TILES if t <= cap and n % t == 0]
        return opts or [n]

    k_opts = [K] if K <= K_RESIDENT else options(K, K_RESIDENT)
    k_opts = [t for t in k_opts if t >= 512] or k_opts
    best = None
    for tk in k_opts:
        for tm in options(M, 2048):
            for tn in options(N, 2048):
                vmem = 2 * (tm * tk * sa + tk * tn * sb + tm * tn * so) + (tm * tn * 4 if tk < K else 0)
                if fused:
                    vmem += 2 * tm * tn * (4 + 2)
                vmem += 2 * tm * tn * 4
                if vmem > MM_VMEM_BUDGET:
                    continue
                a_reads = 1 if tk == K else N // tn
                b_reads = 1 if (tk == K and tn == N) else M // tm
                traffic = M * K * sa * a_reads + K * N * sb * b_reads + M * N * so
                steps = (M // tm) * (N // tn) * (K // tk)
                acc_trips = M * N * 8 * (K // tk) // ACC_DISCOUNT if tk < K else 0
                cost = traffic + acc_trips + steps * STEP_COST_BYTES
                if best is None or cost < best[0]:
                    best = (cost, tm, tn, tk)
    assert best is not None, (M, N, K)
    return best[1:]


def _mm(a, b, *, ta=False, tb=False, out_dtype=F32, resid=None, gate=None, mask=None, name="mm"):
    (K, M) = a.shape if ta else a.shape[::-1]
    N = b.shape[0] if tb else b.shape[1]
    assert (b.shape[1] if tb else b.shape[0]) == K
    fused = resid is not None
    tm, tn, tk = _mm_tiles(M, N, K, a.dtype.itemsize, b.dtype.itemsize, jnp.dtype(out_dtype).itemsize, fused)
    nk = K // tk
    dn = (((0 if ta else 1,), (1 if tb else 0,)), ((), ()))

    def body(*refs):
        refs = list(refs)
        acc_ref = refs.pop() if nk > 1 else None
        if fused:
            a_ref, b_ref, r_ref, g_ref, m_ref, o_ref, y_ref = refs
        else:
            a_ref, b_ref, o_ref = refs
        part = lax.dot_general(a_ref[...].astype(BF16), b_ref[...].astype(BF16), dn, preferred_element_type=F32)

        def finish(acc):
            if fused:
                g = jnp.where(m_ref[...] > 0.5, g_ref[1:2, :], g_ref[0:1, :])
                y_ref[...] = acc.astype(y_ref.dtype)
                o_ref[...] = r_ref[...] + g * acc
            else:
                o_ref[...] = acc.astype(o_ref.dtype)

        if nk == 1:
            finish(part)
        else:
            k = pl.program_id(2)

            @pl.when(k == 0)
            def _():
                acc_ref[...] = part

            @pl.when(k > 0)
            def _():
                acc_ref[...] += part

            @pl.when(k == nk - 1)
            def _():
                finish(acc_ref[...])

    a_spec = pl.BlockSpec((tk, tm), lambda i, j, k: (k, i)) if ta else pl.BlockSpec((tm, tk), lambda i, j, k: (i, k))
    b_spec = pl.BlockSpec((tn, tk), lambda i, j, k: (j, k)) if tb else pl.BlockSpec((tk, tn), lambda i, j, k: (k, j))
    o_spec = pl.BlockSpec((tm, tn), lambda i, j, k: (i, j))
    in_specs, args = [a_spec, b_spec], [a, b]
    out_shape, out_specs = jax.ShapeDtypeStruct((M, N), out_dtype), o_spec
    if fused:
        in_specs += [o_spec, pl.BlockSpec((2, tn), lambda i, j, k: (0, j)), pl.BlockSpec((tm, 1), lambda i, j, k: (i, 0))]
        args += [resid, gate, mask]
        out_shape = (jax.ShapeDtypeStruct((M, N), F32), jax.ShapeDtypeStruct((M, N), BF16))
        out_specs = (o_spec, o_spec)
    return pl.pallas_call(
        body, name=name, grid=(M // tm, N // tn, nk), in_specs=in_specs, out_specs=out_specs, out_shape=out_shape,
        scratch_shapes=[pltpu.VMEM((tm, tn), F32)] if nk > 1 else [],
        compiler_params=_params("parallel", "parallel", "arbitrary"),
    )(*args)


def _make_linear(out_dtype, name):
    @jax.custom_vjp
    def op(a, aT, w):
        return _mm(a, w, out_dtype=out_dtype, name=name)

    def fwd(a, aT, w):
        return op(a, aT, w), (aT, w)

    def bwd(res, dy):
        aT, w = res
        return (_mm(dy, w, tb=True, out_dtype=aT.dtype, name="lin_da"), None,
                _mm(aT, dy, out_dtype=w.dtype, name="lin_dw"))

    op.defvjp(fwd, bwd)
    return op


linear = _make_linear(F32, "lin_fwd")
linear_bf16 = _make_linear(BF16, "linb_fwd")


def _gate_bwd(dx, y, gate, mask):
    T, N = dx.shape
    tb = _pick(T, ROW_TILES)

    def body(dx_ref, y_ref, g_ref, m_ref, dy_ref, dg_ref):
        @pl.when(pl.program_id(0) == 0)
        def _():
            dg_ref[...] = jnp.zeros_like(dg_ref)

        d = dx_ref[...]
        ctx = m_ref[...] > 0.5
        dy_ref[...] = (jnp.where(ctx, g_ref[1:2, :], g_ref[0:1, :]) * d).astype(dy_ref.dtype)
        dyy = d * y_ref[...].astype(F32)
        dg_ref[0:1, :] += jnp.sum(jnp.where(ctx, 0.0, dyy), axis=0, keepdims=True)
        dg_ref[1:2, :] += jnp.sum(jnp.where(ctx, dyy, 0.0), axis=0, keepdims=True)

    row = pl.BlockSpec((tb, N), lambda i: (i, 0))
    small = pl.BlockSpec((2, N), lambda i: (0, 0))
    return pl.pallas_call(
        body, name="gate_bwd", grid=(T // tb,),
        in_specs=[row, row, small, pl.BlockSpec((tb, 1), lambda i: (i, 0))], out_specs=(row, small),
        out_shape=(jax.ShapeDtypeStruct((T, N), BF16), jax.ShapeDtypeStruct((2, N), F32)),
        compiler_params=_params("arbitrary"),
    )(dx, y, gate, mask)


@jax.custom_vjp
def linear_gated(a, aT, w, x, gate, mask):
    return _mm(a, w, resid=x, gate=gate, mask=mask, name="ling_fwd")[0]


def _linear_gated_fwd(a, aT, w, x, gate, mask):
    out, y = _mm(a, w, resid=x, gate=gate, mask=mask, name="ling_fwd")
    return out, (aT, w, y, gate, mask)


def _linear_gated_bwd(res, dx):
    aT, w, y, gate, mask = res
    dy, dgate = _gate_bwd(dx, y, gate, mask)
    return (_mm(dy, w, tb=True, out_dtype=aT.dtype, name="ling_da"), None,
            _mm(aT, dy, out_dtype=w.dtype, name="ling_dw"), dx, dgate, None)


linear_gated.defvjp(_linear_gated_fwd, _linear_gated_bwd)


def _rope_rot(x, quarter):
    lane = lax.broadcasted_iota(jnp.int32, x.shape, 1)
    lo = (lane % (2 * quarter)) < quarter
    return jnp.where(lo, -pltpu.roll(x, LANES - quarter, 1), pltpu.roll(x, quarter, 1))


def _group_mean(v, group):
    if group == LANES:
        return jnp.mean(v, axis=-1, keepdims=True)
    lane = lax.broadcasted_iota(jnp.int32, v.shape, 1)
    lo = lane < group
    s_lo = jnp.sum(jnp.where(lo, v, 0.0), axis=-1, keepdims=True)
    s_hi = jnp.sum(jnp.where(lo, 0.0, v), axis=-1, keepdims=True)
    return jnp.where(lo, s_lo, s_hi) * (1.0 / group)


def _norm_call(x, gain, dy, *, group, quarter, cos, sin, shift, scale, mask, out_dtype, name, dres=None):
    T, W = x.shape
    whole = group == W
    use_rope, use_mod, bwd = cos is not None, shift is not None, dy is not None
    assert not (whole and use_rope) and not (use_mod and not whole)
    tb = _pick(T, ROW_TILES)
    nchunk = W // LANES

    def body(*refs):
        refs = list(refs)
        x_ref, g_ref = refs.pop(0), refs.pop(0)
        dy_ref = refs.pop(0) if bwd else None
        dres_ref = refs.pop(0) if (bwd and use_mod) else None
        cos_ref, sin_ref = (refs.pop(0), refs.pop(0)) if use_rope else (None, None)
        sh_ref, sc_ref, m_ref = (refs.pop(0), refs.pop(0), refs.pop(0)) if use_mod else (None, None, None)
        if not bwd and use_mod:
            o_ref, oT_ref = refs
        elif not bwd:
            (o_ref,) = refs
        elif use_mod:
            dx_ref, dg_ref, dsh_ref, dsc_ref = refs
        else:
            dx_ref, dg_ref = refs

        if bwd:
            @pl.when(pl.program_id(0) == 0)
            def _():
                dg_ref[...] = jnp.zeros_like(dg_ref)
                if use_mod:
                    dsh_ref[...] = jnp.zeros_like(dsh_ref)
                    dsc_ref[...] = jnp.zeros_like(dsc_ref)

        if whole:
            xv = x_ref[...].astype(F32)
            r = lax.rsqrt(jnp.mean(xv * xv, axis=-1, keepdims=True) + EPS)
            xn = xv * r
            gain_v = g_ref[...]
            if use_mod:
                ctx = m_ref[...] > 0.5
                sc = jnp.where(ctx, sc_ref[1:2, :], sc_ref[0:1, :])
            if not bwd:
                y = xn * gain_v
                if use_mod:
                    y = y * (1.0 + sc) + jnp.where(ctx, sh_ref[1:2, :], sh_ref[0:1, :])
                o_ref[...] = y.astype(o_ref.dtype)
                if use_mod:
                    oT_ref[...] = y.T.astype(oT_ref.dtype)
            else:
                d = dy_ref[...].astype(F32)
                if use_mod:
                    dyy = d * (xn * gain_v)
                    dsh_ref[0:1, :] += jnp.sum(jnp.where(ctx, 0.0, d), axis=0, keepdims=True)
                    dsh_ref[1:2, :] += jnp.sum(jnp.where(ctx, d, 0.0), axis=0, keepdims=True)
                    dsc_ref[0:1, :] += jnp.sum(jnp.where(ctx, 0.0, dyy), axis=0, keepdims=True)
                    dsc_ref[1:2, :] += jnp.sum(jnp.where(ctx, dyy, 0.0), axis=0, keepdims=True)
                    d = d * (1.0 + sc)
                dg_ref[...] += jnp.sum(d * xn, axis=0, keepdims=True)
                dxn = d * gain_v
                dxv = r * (dxn - xn * jnp.mean(dxn * xn, axis=-1, keepdims=True))
                if use_mod:
                    dxv = dxv + dres_ref[...]
                dx_ref[...] = dxv.astype(dx_ref.dtype)
        else:
            gain_v = g_ref[...]
            if use_rope:
                cs, sn = cos_ref[...], sin_ref[...]
            for c in range(nchunk):
                cols = slice(c * LANES, (c + 1) * LANES)
                xv = x_ref[:, cols].astype(F32)
                r = lax.rsqrt(_group_mean(xv * xv, group) + EPS)
                xn = xv * r
                if not bwd:
                    y = xn * gain_v
                    if use_rope:
                        y = y * cs + _rope_rot(y, quarter) * sn
                    o_ref[:, cols] = y.astype(o_ref.dtype)
                else:
                    d = dy_ref[:, cols].astype(F32)
                    if use_rope:
                        d = d * cs - _rope_rot(d * sn, quarter)
                    dg_ref[...] += jnp.sum(d * xn, axis=0, keepdims=True)
                    dxn = d * gain_v
                    dx_ref[:, cols] = (r * (dxn - xn * _group_mean(dxn * xn, group))).astype(dx_ref.dtype)

    row = pl.BlockSpec((tb, W), lambda i: (i, 0))
    gw = W if whole else LANES
    gspec = pl.BlockSpec((1, gw), lambda i: (0, 0))
    in_specs, args = [row, gspec], [x, gain]
    if bwd:
        in_specs.append(row)
        args.append(dy)
        if use_mod:
            in_specs.append(row)
            args.append(dres)
    if use_rope:
        tab = pl.BlockSpec((tb, LANES), lambda i: (i, 0))
        in_specs += [tab, tab]
        args += [cos, sin]
    if use_mod:
        two = pl.BlockSpec((2, W), lambda i: (0, 0))
        in_specs += [two, two, pl.BlockSpec((tb, 1), lambda i: (i, 0))]
        args += [shift, scale, mask]
    if not bwd and use_mod:
        out_shape = (jax.ShapeDtypeStruct((T, W), out_dtype), jax.ShapeDtypeStruct((W, T), out_dtype))
        out_specs = (row, pl.BlockSpec((W, tb), lambda i: (0, i)))
    elif not bwd:
        out_shape, out_specs = jax.ShapeDtypeStruct((T, W), out_dtype), row
    else:
        out_shape = [jax.ShapeDtypeStruct((T, W), x.dtype), jax.ShapeDtypeStruct((1, gw), F32)]
        out_specs = [row, gspec]
        if use_mod:
            out_shape += [jax.ShapeDtypeStruct((2, W), F32)] * 2
            out_specs += [pl.BlockSpec((2, W), lambda i: (0, 0))] * 2
    return pl.pallas_call(
        body, name=name, grid=(T // tb,), in_specs=in_specs, out_specs=out_specs, out_shape=out_shape,
        compiler_params=_params("arbitrary" if bwd else "parallel"),
    )(*args)


def make_norm(group, quarter=0, name="norm"):
    kw = dict(group=group, quarter=quarter, shift=None, scale=None, mask=None, out_dtype=BF16)

    @jax.custom_vjp
    def op(x, gain, cos, sin):
        return _norm_call(x, gain, None, cos=cos, sin=sin, name=name + "_fwd", **kw)

    def fwd(x, gain, cos, sin):
        return op(x, gain, cos, sin), (x, gain, cos, sin)

    def bwd(res, dy):
        x, gain, cos, sin = res
        dx, dg = _norm_call(x, gain, dy, cos=cos, sin=sin, name=name + "_bwd", **kw)
        return dx, dg, None, None

    op.defvjp(fwd, bwd)
    return op


@jax.custom_vjp
def modnorm(x, gain, shift, scale, mask):
    h, hT = _norm_call(x, gain, None, group=x.shape[1], quarter=0, cos=None, sin=None, shift=shift, scale=scale,
                       mask=mask, out_dtype=BF16, name="modnorm_fwd")
    return h, hT, x


def _modnorm_fwd(x, gain, shift, scale, mask):
    return modnorm(x, gain, shift, scale, mask), (x, gain, shift, scale, mask)


def _modnorm_bwd(res, cts):
    x, gain, shift, scale, mask = res
    dy, _, dres = cts
    dx, dg, dsh, dsc = _norm_call(x, gain, dy, group=x.shape[1], quarter=0, cos=None, sin=None, shift=shift,
                                  scale=scale, mask=mask, out_dtype=BF16, name="modnorm_bwd", dres=dres)
    return dx, dg, dsh, dsc, None


modnorm.defvjp(_modnorm_fwd, _modnorm_bwd)


def _conv_call(ug, uv, cw, cb, da, *, seq, name):
    T, F = ug.shape
    bwd = da is not None
    tb = _pick(math.gcd(seq, T - seq), ROW_TILES)
    tf = _pick(F, (512, 256, 128))
    nlat, ntot, hpb = seq // tb, T // tb, tb // HALO
    n_ext = tb + 2 * HALO

    def body(*refs):
        if bwd:
            (g_ref, gp_ref, gn_ref, v_ref, vp_ref, vn_ref, d_ref, dp_ref, dn_ref, w_ref, b_ref,
             dg_ref, dv_ref, dw_ref, db_ref) = refs
        else:
            g_ref, gp_ref, gn_ref, v_ref, w_ref, b_ref, o_ref, oT_ref = refs
        i = pl.program_id(1)
        first = jnp.logical_or(i == 0, i == nlat)
        last = jnp.logical_or(i == nlat - 1, i == ntot - 1)
        w = w_ref[...]
        w0, w1, w2, b = w[0:1, :], w[1:2, :], w[2:3, :], b_ref[...]
        if not bwd:
            g = g_ref[...].astype(F32)
            hrow = lax.broadcasted_iota(jnp.int32, (HALO, tf), 0)
            prev = jnp.sum(jnp.where(hrow == HALO - 1, gp_ref[...].astype(F32), 0.0), axis=0, keepdims=True)
            nxt = jnp.sum(jnp.where(hrow == 0, gn_ref[...].astype(F32), 0.0), axis=0, keepdims=True)
            prev = jnp.where(first, 0.0, prev)
            nxt = jnp.where(last, 0.0, nxt)
            row = lax.broadcasted_iota(jnp.int32, g.shape, 0)
            gm1 = jnp.where(row == 0, prev, pltpu.roll(g, 1, 0))
            gp1 = jnp.where(row == tb - 1, nxt, pltpu.roll(g, tb - 1, 0))
            pre = w0 * gm1 + w1 * g + w2 * gp1 + b
            act = pre * jax.nn.sigmoid(pre) * v_ref[...].astype(F32)
            o_ref[...] = act.astype(o_ref.dtype)
            oT_ref[...] = act.T.astype(oT_ref.dtype)
        else:
            @pl.when(i == 0)
            def _():
                dw_ref[...] = jnp.zeros_like(dw_ref)
                db_ref[...] = jnp.zeros_like(db_ref)

            def ext(p_ref, m_ref, n_ref):
                return jnp.concatenate([p_ref[...].astype(F32), m_ref[...].astype(F32), n_ref[...].astype(F32)], axis=0)

            row = lax.broadcasted_iota(jnp.int32, (n_ext, tf), 0)
            valid = jnp.logical_and(jnp.logical_or(row >= HALO, jnp.logical_not(first)),
                                    jnp.logical_or(row < HALO + tb, jnp.logical_not(last)))
            ge = jnp.where(valid, ext(gp_ref, g_ref, gn_ref), 0.0)
            ve = ext(vp_ref, v_ref, vn_ref)
            de = ext(dp_ref, d_ref, dn_ref)
            gm1 = pltpu.roll(ge, 1, 0)
            gp1 = pltpu.roll(ge, n_ext - 1, 0)
            pre = w0 * gm1 + w1 * ge + w2 * gp1 + b
            sg = jax.nn.sigmoid(pre)
            dpre = jnp.where(valid, de * ve * (sg * (1.0 + pre * (1.0 - sg))), 0.0)
            dge = w0 * pltpu.roll(dpre, n_ext - 1, 0) + w1 * dpre + w2 * pltpu.roll(dpre, 1, 0)
            mid = slice(HALO, HALO + tb)
            dg_ref[...] = dge[mid].astype(dg_ref.dtype)
            dv_ref[...] = (de * pre * sg)[mid].astype(dv_ref.dtype)
            dw_ref[0:1, :] += jnp.sum((dpre * gm1)[mid], axis=0, keepdims=True)
            dw_ref[1:2, :] += jnp.sum((dpre * ge)[mid], axis=0, keepdims=True)
            dw_ref[2:3, :] += jnp.sum((dpre * gp1)[mid], axis=0, keepdims=True)
            db_ref[...] += jnp.sum(dpre[mid], axis=0, keepdims=True)

    main = pl.BlockSpec((tb, tf), lambda j, i: (i, j))
    prev = pl.BlockSpec((HALO, tf), lambda j, i: (jnp.maximum(i * hpb - 1, 0), j))
    nxt = pl.BlockSpec((HALO, tf), lambda j, i: (jnp.minimum((i + 1) * hpb, T // HALO - 1), j))
    wspec = pl.BlockSpec((3, tf), lambda j, i: (0, j))
    bspec = pl.BlockSpec((1, tf), lambda j, i: (0, j))
    if not bwd:
        in_specs, args = [main, prev, nxt, main, wspec, bspec], [ug, ug, ug, uv, cw, cb]
        out_shape = (jax.ShapeDtypeStruct((T, F), BF16), jax.ShapeDtypeStruct((F, T), BF16))
        out_specs = (main, pl.BlockSpec((tf, tb), lambda j, i: (j, i)))
    else:
        in_specs = [main, prev, nxt] * 3 + [wspec, bspec]
        args = [ug, ug, ug, uv, uv, uv, da, da, da, cw, cb]
        out_shape = (jax.ShapeDtypeStruct((T, F), BF16), jax.ShapeDtypeStruct((T, F), BF16),
                     jax.ShapeDtypeStruct((3, F), F32), jax.ShapeDtypeStruct((1, F), F32))
        out_specs = (main, main, wspec, bspec)
    return pl.pallas_call(
        body, name=name, grid=(F // tf, T // tb), in_specs=in_specs, out_specs=out_specs, out_shape=out_shape,
        compiler_params=_params("parallel", "arbitrary" if bwd else "parallel"),
    )(*args)


def make_convgate(seq):
    @jax.custom_vjp
    def op(ug, uv, cw, cb):
        return _conv_call(ug, uv, cw, cb, None, seq=seq, name="conv_fwd")

    def fwd(ug, uv, cw, cb):
        return op(ug, uv, cw, cb), (ug, uv, cw, cb)

    def bwd(res, cts):
        ug, uv, cw, cb = res
        return _conv_call(ug, uv, cw, cb, cts[0], seq=seq, name="conv_bwd")

    op.defvjp(fwd, bwd)
    return op


def _attn_tiles(q_len, k_len, bwd):
    if bwd:
        return _pick(q_len, (1024, 512, 256, 128)), _pick(k_len, (768, 512, 256, 128))
    return _pick(q_len, (512, 256, 128)), _pick(k_len, (1408, 768, 512, 256, 128))


def _rows_spec(hm, t, d, head, blk):
    if hm:
        return pl.BlockSpec((None, t, d), lambda h, g, i, j: (head(h, g), blk(i, j), 0))
    return pl.BlockSpec((t, d), lambda h, g, i, j: (blk(i, j), head(h, g)))


def _cols_spec(hm, t, d, head, blk):
    if hm:
        return pl.BlockSpec((None, d, t), lambda h, g, i, j: (head(h, g), 0, blk(i, j)))
    return pl.BlockSpec((d, t), lambda h, g, i, j: (head(h, g), blk(i, j)))


LOG2E = 1.4426950408889634


def _key_parts(tk):
    groups = tk // LANES
    if groups < 2:
        return [(0, tk)]
    cut = (groups + 1) // 2 * LANES
    return [(0, cut), (cut, tk)]


def _attn_fwd_call(qT, k, vT, prev, *, hm, Hk, G, d, dv, scale, q0, q_len, k0, k_len, total, name):
    tq, tk = _attn_tiles(q_len, k_len, False)
    nq, nk, H = q_len // tq, k_len // tk, Hk * G
    qoff, koff = q0 // tq, k0 // tk
    c2 = scale * LOG2E
    parts = _key_parts(tk)

    def body(*refs):
        qT_ref, k_ref, vT_ref = refs[:3]
        oT_ref, o_ref, lse_ref, m_s, l_s, acc_s = refs[-6:]
        kj = pl.program_id(3)

        @pl.when(kj == 0)
        def _():
            m_s[...] = jnp.full_like(m_s, -jnp.inf)
            l_s[...] = jnp.zeros_like(l_s)
            acc_s[...] = jnp.zeros_like(acc_s)

        qT_v = qT_ref[...]
        sTs = [jnp.dot(k_ref[a:b, :], qT_v, preferred_element_type=F32) for a, b in parts]
        m_run, l_run, acc = m_s[...], l_s[...], acc_s[...]
        for (a, b), sT in zip(parts, sTs):
            m_new = jnp.maximum(m_run, jnp.max(sT, axis=0, keepdims=True))
            alpha = jnp.exp2((m_run - m_new) * c2)
            pT = jnp.exp2((sT - m_new) * c2)
            l_run = alpha * l_run + jnp.sum(pT, axis=0, keepdims=True)
            acc = alpha * acc + jnp.dot(vT_ref[:, a:b], pT.astype(BF16), preferred_element_type=F32)
            m_run = m_new
        m_s[...], l_s[...], acc_s[...] = m_run, l_run, acc

        @pl.when(kj == nk - 1)
        def _():
            res = acc_s[...] / l_s[...]
            oT_ref[...] = res.astype(oT_ref.dtype)
            o_ref[...] = res.T.astype(o_ref.dtype)
            lse_ref[...] = m_s[...] * scale + jnp.log(l_s[...])

    qh, kh = (lambda h, g: h * G + g), (lambda h, g: h)
    qb, kb = (lambda i, j: qoff + i), (lambda i, j: koff + j)
    in_specs = [_cols_spec(hm, tq, d, qh, qb), _rows_spec(hm, tk, d, kh, kb), _cols_spec(False, tk, dv, kh, kb)]
    args, alias = [qT, k, vT], {}
    if prev is not None:
        in_specs += [HBM_SPEC, HBM_SPEC, HBM_SPEC]
        args += list(prev)
        alias = {3: 0, 4: 1, 5: 2}
    return pl.pallas_call(
        body, name=name, grid=(Hk, G, nq, nk), in_specs=in_specs,
        out_specs=(_cols_spec(False, tq, dv, qh, qb), _rows_spec(False, tq, dv, qh, qb),
                   pl.BlockSpec((None, 1, tq), lambda h, g, i, j: (h * G + g, 0, qoff + i))),
        out_shape=(jax.ShapeDtypeStruct((H * dv, total), BF16), jax.ShapeDtypeStruct((total, H * dv), BF16),
                   jax.ShapeDtypeStruct((H, 1, total), F32)),
        scratch_shapes=[pltpu.VMEM((1, tq), F32), pltpu.VMEM((1, tq), F32), pltpu.VMEM((dv, tq), F32)],
        input_output_aliases=alias,
        compiler_params=_params("parallel", "parallel", "parallel", "arbitrary"),
    )(*args)


def _attn_bwd_call(qT, q, k, kT, v, doT, do, oT, lse, prev, *, hm, Hk, G, d, dv, scale, q0, q_len, k0, k_len, total, name):
    tq, tk = _attn_tiles(q_len, k_len, True)
    nq, nk, H = q_len // tq, k_len // tk, Hk * G
    qoff, koff = q0 // tq, k0 // tk
    c2 = scale * LOG2E
    acc_in = prev is not None

    def body(*refs):
        qT_ref, q_ref, k_ref, kT_ref, v_ref, doT_ref, do_ref, oT_ref, lse_ref = refs[:9]
        dqT_ref, dk_ref, dv_ref, dq_s, delta_s = refs[-5:]
        g, qi, kj = pl.program_id(1), pl.program_id(2), pl.program_id(3)

        @pl.when(jnp.logical_and(jnp.logical_and(g == 0, qi == 0), kj == 0))
        def _():
            if acc_in:
                dk_ref[...] = refs[10][...]
                dv_ref[...] = refs[11][...]
            else:
                dk_ref[...] = jnp.zeros_like(dk_ref)
                dv_ref[...] = jnp.zeros_like(dv_ref)

        doT_v = doT_ref[...]

        @pl.when(kj == 0)
        def _():
            delta_s[...] = jnp.sum(doT_v.astype(F32) * oT_ref[...].astype(F32), axis=0, keepdims=True)
            dq_s[...] = jnp.zeros_like(dq_s)

        sT = jnp.dot(k_ref[...], qT_ref[...], preferred_element_type=F32)
        pT = jnp.exp2(sT * c2 - lse_ref[...] * LOG2E)
        rows = pl.ds(pl.multiple_of(kj * tk, tk), tk)
        dv_ref[rows, :] += jnp.dot(pT.astype(BF16), do_ref[...], preferred_element_type=F32)
        dpT = jnp.dot(v_ref[...], doT_v, preferred_element_type=F32)
        dsT = (pT * (dpT - delta_s[...]) * scale).astype(BF16)
        dk_ref[rows, :] += jnp.dot(dsT, q_ref[...], preferred_element_type=F32)
        dq_s[...] += jnp.dot(kT_ref[...], dsT, preferred_element_type=F32)

        @pl.when(kj == nk - 1)
        def _():
            dqT_ref[...] = dq_s[...].astype(dqT_ref.dtype)

    qh, kh = (lambda h, g: h * G + g), (lambda h, g: h)
    qb, kb = (lambda i, j: qoff + i), (lambda i, j: koff + j)
    kres = lambda i, j: k0 // k_len
    in_specs = [_cols_spec(hm, tq, d, qh, qb), _rows_spec(hm, tq, d, qh, qb), _rows_spec(hm, tk, d, kh, kb),
                _cols_spec(hm, tk, d, kh, kb), _rows_spec(False, tk, dv, kh, kb), _cols_spec(False, tq, dv, qh, qb),
                _rows_spec(False, tq, dv, qh, qb), _cols_spec(False, tq, dv, qh, qb),
                pl.BlockSpec((None, 1, tq), lambda h, g, i, j: (h * G + g, 0, qoff + i))]
    args, alias = [qT, q, k, kT, v, doT, do, oT, lse], {}
    dk_spec, dv_spec = _rows_spec(hm, k_len, d, kh, kres), _rows_spec(False, k_len, dv, kh, kres)
    if acc_in:
        in_specs += [HBM_SPEC, dk_spec, dv_spec]
        args += list(prev)
        alias = {9: 0, 10: 1, 11: 2}
    dq_shape = (H, d, total) if hm else (H * d, total)
    dk_shape = (Hk, total, d) if hm else (total, Hk * d)
    return pl.pallas_call(
        body, name=name, grid=(Hk, G, nq, nk), in_specs=in_specs,
        out_specs=(_cols_spec(hm, tq, d, qh, qb), dk_spec, dv_spec),
        out_shape=(jax.ShapeDtypeStruct(dq_shape, BF16), jax.ShapeDtypeStruct(dk_shape, F32),
                   jax.ShapeDtypeStruct((total, Hk * dv), F32)),
        scratch_shapes=[pltpu.VMEM((d, tq), F32), pltpu.VMEM((1, tq), F32)],
        input_output_aliases=alias,
        compiler_params=_params("parallel", "arbitrary", "arbitrary", "arbitrary"),
    )(*args)


def make_attention(*, hm, Hk, G, d, dv, scale, seq, total, name):
    kw = dict(hm=hm, Hk=Hk, G=G, d=d, dv=dv, scale=scale, total=total)
    lat = dict(q0=0, q_len=seq, k0=0, k_len=total)
    ctx = dict(q0=seq, q_len=total - seq, k0=seq, k_len=total - seq)
    tr = (lambda a: jnp.swapaxes(a, 1, 2)) if hm else (lambda a: a.T)

    def run(q, k, v):
        qT, vT = tr(q), v.T
        first = _attn_fwd_call(qT, k, vT, None, name=name + "_fwd_lat", **kw, **lat)
        return _attn_fwd_call(qT, k, vT, first, name=name + "_fwd_ctx", **kw, **ctx)

    @jax.custom_vjp
    def op(q, k, v):
        oT, o, _ = run(q, k, v)
        return o, oT

    def fwd(q, k, v):
        oT, o, lse = run(q, k, v)
        return (o, oT), (q, k, v, oT, lse)

    def bwd(res, cts):
        q, k, v, oT, lse = res
        do = cts[0]
        ops = (tr(q), q, k, tr(k), v, do.T, do, oT, lse)
        first = _attn_bwd_call(*ops, None, name=name + "_bwd_lat", **kw, **lat)
        dqT, dk, dv_ = _attn_bwd_call(*ops, first, name=name + "_bwd_ctx", **kw, **ctx)
        return tr(dqT).astype(q.dtype), dk.astype(k.dtype), dv_.astype(v.dtype)

    op.defvjp(fwd, bwd)
    return op


def _loss_call(xf, target):
    S, D = target.shape
    tb = _pick(S, ROW_TILES)

    def body(y_ref, t_ref, dy_ref, acc_ref):
        @pl.when(pl.program_id(0) == 0)
        def _():
            acc_ref[...] = jnp.zeros_like(acc_ref)

        e = y_ref[...] - t_ref[...]
        dy_ref[...] = e * (1.0 / D)
        acc_ref[...] += jnp.sum(e * e, axis=0, keepdims=True)

    row = pl.BlockSpec((tb, D), lambda i: (i, 0))
    return pl.pallas_call(
        body, name="loss", grid=(S // tb,), in_specs=[row, row],
        out_specs=(row, pl.BlockSpec((1, D), lambda i: (0, 0))),
        out_shape=(jax.ShapeDtypeStruct((S, D), F32), jax.ShapeDtypeStruct((1, D), F32)),
        compiler_params=_params("arbitrary"),
    )(xf, target)


def _adamw_call(w, g, m, v, name):
    R, N = w.shape
    tb = _pick(R, tuple(t for t in (512, 256, 128, 64, 32, 16, 8) if t * N * 4 <= (1 << 20)) or (8,))
    bc1, bc2 = 1.0 - ADAM_B1 ** ADAM_STEP, 1.0 - ADAM_B2 ** ADAM_STEP

    def body(w_ref, g_ref, m_ref, v_ref, d_ref, nm_ref, nv_ref):
        gv = g_ref[...]
        nm = ADAM_B1 * m_ref[...] + (1.0 - ADAM_B1) * gv
        nv = ADAM_B2 * v_ref[...] + (1.0 - ADAM_B2) * (gv * gv)
        d_ref[...] = -ADAM_LR * ((nm / bc1) / (jnp.sqrt(nv / bc2) + ADAM_EPS) + ADAM_WD * w_ref[...])
        nm_ref[...] = nm
        nv_ref[...] = nv

    spec = pl.BlockSpec((tb, N), lambda i: (i, 0))
    shp = jax.ShapeDtypeStruct((R, N), F32)
    return pl.pallas_call(
        body, name=name, grid=(R // tb,), in_specs=[spec] * 4, out_specs=(spec,) * 3, out_shape=(shp,) * 3,
        compiler_params=_params("parallel"),
    )(w, g, m, v)


def _adamw(w, g, m, v, name="adamw"):
    shape = w.shape
    two = (lambda a: a.reshape(-1, shape[-1])) if w.ndim >= 2 else (lambda a: a.reshape(1, -1))
    return tuple(r.reshape(shape) for r in _adamw_call(two(w), two(g), two(m), two(v), name))


def _me():
    return lax.axis_index("x"), lax.axis_index("y"), lax.axis_index("c")


def _gather8(x, *, reduce, name):
    R, W = x.shape

    def body(x_ref, out_ref, *scratch):
        if reduce:
            buf, send_sems, recv_sems, local_sem = scratch
        else:
            buf = out_ref
            send_sems, recv_sems, local_sem = scratch
        x, y, c = _me()
        me, sibling = (x, y, c), (x, y, 1 - c)
        chips = [(1 - x, y), (x, 1 - y), (1 - x, 1 - y)]

        def rows(px, py, pc):
            return buf.at[pl.ds((4 * px + 2 * py + pc) * R, R), :]

        def copy(k, block, to, src=None):
            return pltpu.make_async_remote_copy(
                src_ref=rows(*block) if src is None else src, dst_ref=rows(*block),
                send_sem=send_sems.at[k], recv_sem=recv_sems.at[k], device_id=to, device_id_type=MESH)

        mine = pltpu.make_async_copy(x_ref, rows(*me), local_sem)
        mine.start()
        first = [copy(0, me, sibling, src=x_ref)]
        first += [copy(1 + j, me, (*chip, c), src=x_ref) for j, chip in enumerate(chips)]
        for cp in first:
            cp.start()
        passed = [copy(4 + j, (*chip, c), sibling) for j, chip in enumerate(chips)]
        for j, chip in enumerate(chips):
            copy(1 + j, (*chip, c), me).wait_recv()
            passed[j].start()
        copy(0, sibling, me).wait_recv()
        for j, chip in enumerate(chips):
            copy(4 + j, (*chip, 1 - c), me).wait_recv()
        for cp in first + passed:
            cp.wait_send()
        mine.wait()
        if reduce:
            acc = buf[0:R, :]
            for e in range(1, 8):
                acc = acc + buf[e * R:(e + 1) * R, :]
            out_ref[...] = acc

    sems = [pltpu.SemaphoreType.DMA((7,)), pltpu.SemaphoreType.DMA((7,)), pltpu.SemaphoreType.DMA]
    return pl.pallas_call(
        body, name=name,
        out_shape=jax.ShapeDtypeStruct((R if reduce else 8 * R, W), F32),
        in_specs=[pl.BlockSpec(memory_space=pltpu.VMEM)], out_specs=pl.BlockSpec(memory_space=pltpu.VMEM),
        scratch_shapes=([pltpu.VMEM((8 * R, W), F32)] if reduce else []) + sems,
        compiler_params=pltpu.CompilerParams(vmem_limit_bytes=VMEM_LIMIT),
    )(x)


def _small_exchange(flat, *, reduce, name):
    n = flat.shape[0]
    W = PACK_W if n >= 8 * PACK_W else LANES
    R = -(-n // (8 * W)) * 8
    x = jnp.pad(flat, (0, R * W - n)).reshape(R, W)
    out = _gather8(x, reduce=reduce, name=name)
    if reduce:
        return out.reshape(-1)[:n]
    return out.reshape(8, R * W)[:, :n]


HBM_SPEC = pl.BlockSpec(memory_space=pl.ANY)


def _chip_gather(mine):
    _, R, W = mine.shape

    def body(m_ref, out_ref, send_sems, recv_sems):
        x, y, c = _me()
        sibling = (x, y, 1 - c)
        chips = [(1 - x, y), (x, 1 - y), (1 - x, 1 - y)]

        def slot(px, py, half):
            return out_ref.at[half, 2 * px + py]

        def copy(k, src, dst, to):
            return pltpu.make_async_remote_copy(src_ref=src, dst_ref=dst, send_sem=send_sems.at[k],
                                                recv_sem=recv_sems.at[k], device_id=to, device_id_type=MESH)

        first = [copy(j, m_ref.at[c], slot(x, y, c), (*chip, c)) for j, chip in enumerate(chips)]
        for cp in first:
            cp.start()
        passed = [copy(3 + j, slot(*chip, c), slot(*chip, c), sibling) for j, chip in enumerate(chips)]
        for j, chip in enumerate(chips):
            copy(j, m_ref.at[c], slot(*chip, c), (*chip, c)).wait_recv()
            passed[j].start()
        for j, chip in enumerate(chips):
            copy(3 + j, slot(*chip, 1 - c), slot(*chip, 1 - c), sibling).wait_recv()
        for cp in first + passed:
            cp.wait_send()

    out = pl.pallas_call(
        body, name="chip_gather", out_shape=jax.ShapeDtypeStruct((2, 4, R, W), mine.dtype),
        in_specs=[HBM_SPEC], out_specs=HBM_SPEC,
        scratch_shapes=[pltpu.SemaphoreType.DMA((6,)), pltpu.SemaphoreType.DMA((6,))],
    )(mine)
    x, y, _ = _me()
    return lax.dynamic_update_slice(out, mine[:, None], (0, 2 * x + y, 0, 0))


def _pair_swap(buf):
    def body(b_ref, out_ref, send_sem, recv_sem):
        x, y, c = _me()
        cp = pltpu.make_async_remote_copy(src_ref=b_ref.at[1 - c], dst_ref=out_ref, send_sem=send_sem,
                                          recv_sem=recv_sem, device_id=(x, y, 1 - c), device_id_type=MESH)
        cp.start()
        cp.wait()

    return pl.pallas_call(
        body, name="pair_swap_grads", out_shape=jax.ShapeDtypeStruct(buf.shape[1:], buf.dtype), in_specs=[HBM_SPEC],
        out_specs=HBM_SPEC, scratch_shapes=[pltpu.SemaphoreType.DMA, pltpu.SemaphoreType.DMA],
    )(buf)


def _pair_share(half):
    def body(h_ref, out_ref, send_sem, recv_sem):
        x, y, c = _me()
        cp = pltpu.make_async_remote_copy(src_ref=h_ref, dst_ref=out_ref.at[c], send_sem=send_sem, recv_sem=recv_sem,
                                          device_id=(x, y, 1 - c), device_id_type=MESH)
        cp.start()
        pltpu.make_async_remote_copy(src_ref=h_ref, dst_ref=out_ref.at[1 - c], send_sem=send_sem, recv_sem=recv_sem,
                                     device_id=(x, y, 1 - c), device_id_type=MESH).wait_recv()
        cp.wait_send()

    out = pl.pallas_call(
        body, name="pair_share", out_shape=jax.ShapeDtypeStruct((2,) + half.shape, half.dtype), in_specs=[HBM_SPEC],
        out_specs=HBM_SPEC, scratch_shapes=[pltpu.SemaphoreType.DMA, pltpu.SemaphoreType.DMA],
    )(half)
    return lax.dynamic_update_slice(out, half[None], (lax.axis_index("c"), 0, 0))


def _chip_exchange(s):
    def body(s_ref, out_ref, send_sems, recv_sems):
        x, y, c = _me()
        q = 2 * x + y
        chips = [(1 - x, y), (x, 1 - y), (1 - x, 1 - y)]

        def copy(j, chip):
            return pltpu.make_async_remote_copy(
                src_ref=s_ref.at[2 * chip[0] + chip[1]], dst_ref=out_ref.at[q], send_sem=send_sems.at[j],
                recv_sem=recv_sems.at[j], device_id=(*chip, c), device_id_type=MESH)

        sends = [copy(j, chip) for j, chip in enumerate(chips)]
        for cp in sends:
            cp.start()
        for j, chip in enumerate(chips):
            pltpu.make_async_remote_copy(
                src_ref=s_ref.at[q], dst_ref=out_ref.at[2 * chip[0] + chip[1]], send_sem=send_sems.at[j],
                recv_sem=recv_sems.at[j], device_id=(*chip, c), device_id_type=MESH).wait_recv()
        for cp in sends:
            cp.wait_send()

    return pl.pallas_call(
        body, name="chip_exchange", out_shape=jax.ShapeDtypeStruct(s.shape, s.dtype), in_specs=[HBM_SPEC],
        out_specs=HBM_SPEC,
        scratch_shapes=[pltpu.SemaphoreType.DMA((3,)), pltpu.SemaphoreType.DMA((3,))],
    )(s)


def _pair_sum(p, got, c):
    _, _, R, W = p.shape
    tb = _pick(R, ROW_TILES)

    def body(c_ref, p_ref, g_ref, o_ref):
        o_ref[...] = (p_ref[...].astype(F32) + g_ref[...].astype(F32)).astype(o_ref.dtype)

    return pl.pallas_call(
        body, name="pair_sum",
        grid_spec=pltpu.PrefetchScalarGridSpec(
            num_scalar_prefetch=1, grid=(4, R // tb),
            in_specs=[pl.BlockSpec((None, None, tb, W), lambda s, i, cr: (cr[0], s, i, 0)),
                      pl.BlockSpec((None, tb, W), lambda s, i, cr: (s, i, 0))],
            out_specs=pl.BlockSpec((None, tb, W), lambda s, i, cr: (s, i, 0))),
        out_shape=jax.ShapeDtypeStruct((4, R, W), BF16),
        compiler_params=_params("parallel", "parallel"),
    )(jnp.reshape(c, (1,)).astype(jnp.int32), p, got)


def _sum4(r, s, q):
    _, R, W = r.shape
    tb = _pick(R, ROW_TILES)

    def body(q_ref, r_ref, s_ref, o_ref):
        own = s_ref[...].astype(F32)
        acc = jnp.where(q_ref[0] == 0, own, r_ref[0].astype(F32))
        for a in range(1, 4):
            acc = acc + jnp.where(q_ref[0] == a, own, r_ref[a].astype(F32))
        o_ref[...] = acc

    return pl.pallas_call(
        body, name="sum4",
        grid_spec=pltpu.PrefetchScalarGridSpec(
            num_scalar_prefetch=1, grid=(R // tb,),
            in_specs=[pl.BlockSpec((4, tb, W), lambda i, qr: (0, i, 0)),
                      pl.BlockSpec((None, tb, W), lambda i, qr: (qr[0], i, 0))],
            out_specs=pl.BlockSpec((tb, W), lambda i, qr: (i, 0))),
        out_shape=jax.ShapeDtypeStruct((R, W), F32),
        compiler_params=_params("parallel"),
    )(jnp.reshape(q, (1,)).astype(jnp.int32), r, s)


BIG = (("mla_w_dq", 1), ("mla_w_uq", 2), ("mla_w_dkv", 1), ("mla_w_ukv", 2), ("mla_w_o", 1),
       ("gqa_w_q", 1), ("gqa_w_kv", 1), ("gqa_w_o", 1), ("ffn_w_up", 2), ("ffn_w_down", 1))


def _pack_rows(n):
    return -(-n // (2 * PACK_W * ROW_TILES[0])) * ROW_TILES[0]


def _to_shards(full, axis):
    L, K, N = full.shape
    if axis == 1:
        return full.reshape(L, 4, K // 4, N).transpose(1, 0, 2, 3).reshape(4, -1)
    return full.reshape(L, K, 4, N // 4).transpose(2, 0, 1, 3).reshape(4, -1)


def _from_shards(flat4, shard_shape, axis):
    L, K, N = shard_shape
    a = flat4.reshape(4, L, K, N)
    if axis == 1:
        return a.transpose(1, 0, 2, 3).reshape(L, 4 * K, N)
    return a.transpose(1, 2, 0, 3).reshape(L, K, 4 * N)


def _rope_tables(seq, total, rot_dim):
    t = jnp.arange(seq, dtype=jnp.int32)
    rows, cols = t // GRID_W, t % GRID_W
    axis_dim = rot_dim // 2
    inv = jnp.power(ROPE_BASE, -jnp.arange(0, axis_dim, 2, dtype=F32) / axis_dim)
    ang_r = rows.astype(F32)[:, None] * inv
    ang_c = cols.astype(F32)[:, None] * inv
    ang = jnp.concatenate([ang_r, ang_r, ang_c, ang_c], axis=-1)
    cos = jnp.concatenate([jnp.cos(ang), jnp.ones((total - seq, rot_dim), F32)], axis=0)
    sin = jnp.concatenate([jnp.sin(ang), jnp.zeros((total - seq, rot_dim), F32)], axis=0)
    rep = LANES // rot_dim
    return jnp.tile(cos, (1, rep)), jnp.tile(sin, (1, rep))


def _forward(X, mod, P, *, seq):
    T, D = X.shape
    depth = mod.shape[0]
    mask = (jnp.arange(T) >= seq).astype(F32)[:, None]
    rope_m = _rope_tables(seq, T, ROPE_D)
    rope_g = _rope_tables(seq, T, HEAD)
    Hm = P["mla_w_uq"].shape[-1] // (NOPE + ROPE_D)
    Hg = P["gqa_w_q"].shape[-1] // HEAD
    Hkv = P["gqa_w_kv"].shape[-1] // (2 * HEAD)
    kv_rank = P["mla_w_dkv"].shape[-1] - ROPE_D
    dff = P["ffn_conv_b"].shape[-1]

    norm_row = make_norm(P["mla_w_dq"].shape[-1], name="norm_rank")
    norm_kv = make_norm(kv_rank, name="norm_kvrank")
    norm_head = make_norm(LANES, name="norm_head")
    norm_pe = make_norm(ROPE_D, ROPE_D // 4, name="norm_pe")
    norm_gqa = make_norm(HEAD, HEAD // 4, name="norm_gqa")
    attn_mla = make_attention(hm=True, Hk=Hm, G=1, d=NOPE + ROPE_D, dv=VDIM, scale=1.0 / math.sqrt(NOPE + ROPE_D),
                              seq=seq, total=T, name="mla")
    attn_gqa = make_attention(hm=False, Hk=Hkv, G=Hg // Hkv, d=HEAD, dv=HEAD, scale=1.0 / math.sqrt(HEAD),
                              seq=seq, total=T, name="gqa")
    convgate = make_convgate(seq)
    row = lambda g: g.reshape(1, -1)
    twice = lambda g: jnp.concatenate([g, g]).reshape(1, -1)

    for i in range(depth):
        j = i // 2
        sh1, sc1, g1, sh2, sc2, g2 = [mod[i][:, k * D:(k + 1) * D] for k in range(N_MOD)]
        h, hT, X = modnorm(X, row(P["norm_mix"][i]), sh1, sc1, mask)
        if i % 2 == 0:
            w_uq = P["mla_w_uq"][j].reshape(-1, Hm, NOPE + ROPE_D)
            w_ukv = P["mla_w_ukv"][j].reshape(-1, Hm, NOPE + VDIM)
            cq = norm_row(linear(h, hT, P["mla_w_dq"][j]), row(P["mla_g_dq"][j]), None, None)
            cqT = cq.T
            qn = norm_head(linear(cq, cqT, w_uq[:, :, :NOPE].reshape(-1, Hm * NOPE)), row(P["mla_g_q_nope"][j]), None, None)
            qp = norm_pe(linear(cq, cqT, w_uq[:, :, NOPE:].reshape(-1, Hm * ROPE_D)), twice(P["mla_g_q_pe"][j]), *rope_m)
            kva = linear(h, hT, P["mla_w_dkv"][j])
            ckv = norm_kv(kva[:, :kv_rank], row(P["mla_g_dkv"][j]), None, None)
            ckvT = ckv.T
            kp_raw = kva[:, kv_rank:]
            kp = norm_pe(jnp.concatenate([kp_raw, kp_raw], axis=-1), twice(P["mla_g_k_pe"][j]), *rope_m)[:, :ROPE_D]
            kn = norm_head(linear(ckv, ckvT, w_ukv[:, :, :NOPE].reshape(-1, Hm * NOPE)), row(P["mla_g_k_nope"][j]), None, None)
            v = linear_bf16(ckv, ckvT, w_ukv[:, :, NOPE:].reshape(-1, Hm * VDIM))
            q_hm = jnp.concatenate([qn.reshape(T, Hm, NOPE), qp.reshape(T, Hm, ROPE_D)], axis=-1).transpose(1, 0, 2)
            k_hm = jnp.concatenate([kn.reshape(T, Hm, NOPE), jnp.broadcast_to(kp[:, None, :], (T, Hm, ROPE_D))],
                                   axis=-1).transpose(1, 0, 2)
            o, oT = attn_mla(q_hm, k_hm, v)
            X = linear_gated(o, oT, P["mla_w_o"][j], X, g1, mask)
        else:
            w_kv = P["gqa_w_kv"][j]
            q = norm_gqa(linear(h, hT, P["gqa_w_q"][j]), row(P["gqa_g_q"][j]), *rope_g)
            k = norm_gqa(linear(h, hT, w_kv[:, :Hkv * HEAD]), row(P["gqa_g_k"][j]), *rope_g)
            v = linear_bf16(h, hT, w_kv[:, Hkv * HEAD:])
            o, oT = attn_gqa(q, k, v)
            X = linear_gated(o, oT, P["gqa_w_o"][j], X, g1, mask)
        h2, h2T, X = modnorm(X, row(P["norm_ffn"][i]), sh2, sc2, mask)
        w_up = P["ffn_w_up"][i]
        ug = linear_bf16(h2, h2T, w_up[:, :dff])
        uv = linear_bf16(h2, h2T, w_up[:, dff:])
        a, aT = convgate(ug, uv, P["ffn_conv_w"][i], row(P["ffn_conv_b"][i]))
        X = linear_gated(a, aT, P["ffn_w_down"][i], X, g2, mask)
    return X


SMALL = ("norm_mix", "norm_ffn", "mla_g_dq", "mla_g_q_nope", "mla_g_q_pe", "mla_g_dkv", "mla_g_k_pe",
         "mla_g_k_nope", "gqa_g_q", "gqa_g_k", "ffn_conv_w", "ffn_conv_b")
WEIGHTS = ("c_ctx", "w_mod", "b_mod", "norm_mix", "norm_ffn", "mla_w_dq", "mla_g_dq", "mla_w_uq", "mla_g_q_nope",
           "mla_g_q_pe", "mla_w_dkv", "mla_g_dkv", "mla_g_k_pe", "mla_w_ukv", "mla_g_k_nope", "mla_w_o", "gqa_w_q",
           "gqa_g_q", "gqa_w_kv", "gqa_g_k", "gqa_w_o", "ffn_w_up", "ffn_conv_w", "ffn_conv_b", "ffn_w_down")


def _step(A):
    x, y, c = _me()
    q = 2 * x + y
    dev = 4 * x + 2 * y + c
    seq, D = A["x"].shape[1], A["x"].shape[2]
    depth = A["w_mod"].shape[0]
    X0 = jnp.concatenate([A["x"][0], A["ctx"][0]], axis=0)

    shard_n = [math.prod(A[n].shape) for n, _ in BIG]
    n_tot = sum(shard_n)
    R = _pack_rows(n_tot)
    flat = jnp.concatenate([A[n].astype(BF16).reshape(-1) for n, _ in BIG])
    mine = jnp.pad(flat, (0, 2 * R * PACK_W - n_tot)).reshape(2, R, PACK_W)
    allw = _chip_gather(mine).transpose(1, 0, 2, 3).reshape(4, -1)
    P, off = {}, 0
    for (n, axis), cnt in zip(BIG, shard_n):
        P[n] = _from_shards(allw[:, off:off + cnt], A[n].shape, axis)
        off += cnt

    c_all = _small_exchange(A["c"].reshape(-1), reduce=False, name="gather_c")
    cw = _small_exchange(A["ffn_conv_w"].reshape(-1), reduce=False, name="gather_convw")
    cw = cw.reshape(4, 2, depth, 3, -1)[:, 0].transpose(1, 2, 0, 3).reshape(depth, 3, -1)
    cvec = jnp.concatenate([c_all, jnp.broadcast_to(A["c_ctx"][None, :], (8, D))], axis=0)
    act = jax.nn.silu(cvec)
    ncol = A["w_mod"].shape[-1]
    modcols = jnp.stack([_mm(act, A["w_mod"][i], name="mod_fwd") for i in range(depth)])
    mod_all = _small_exchange(modcols.reshape(-1), reduce=False, name="gather_mod")
    mod_all = mod_all.reshape(4, 2, depth, 16, ncol)[:, 0].transpose(1, 2, 0, 3).reshape(depth, 16, 4 * ncol)
    mod_all = mod_all + A["b_mod"][:, None, :]
    mod = jnp.stack([lax.dynamic_index_in_dim(mod_all, dev, 1, keepdims=False), mod_all[:, 8]], axis=1)

    for n in SMALL:
        P[n] = cw if n == "ffn_conv_w" else A[n]

    XF, vjp = jax.vjp(functools.partial(_forward, seq=seq), X0, mod, P)
    dy, sq = _loss_call(XF, A["loss_target"][0])
    loss_part = (0.5 / D * jnp.sum(sq)).reshape(1)
    dX0, dmod, dP = vjp(jnp.concatenate([dy, jnp.zeros((X0.shape[0] - seq, D), F32)], axis=0))
    grad_x = dX0[:seq][None]

    G = {}
    dmod_all = _small_exchange(dmod.reshape(-1), reduce=False, name="gather_dmod").reshape(8, depth, 2, -1)
    drows = jnp.concatenate([dmod_all[:, :, 0], dmod_all[:, :, 1]], axis=0).transpose(1, 0, 2)
    G["b_mod"] = jnp.sum(drows, axis=1)
    dcols = lax.dynamic_slice_in_dim(drows, q * ncol, ncol, axis=2)
    G["w_mod"] = jnp.stack([_mm(act, dcols[i], ta=True, name="mod_dw") for i in range(depth)])
    dact = sum(_mm(dcols[i], A["w_mod"][i], tb=True, name="mod_dact") for i in range(depth))
    dcc_part = 0.5 * jnp.sum(dact[8:], axis=0)

    small_list = [dP[n].astype(F32).reshape(-1) for n in SMALL] + [dcc_part]
    sizes = [a.shape[0] for a in small_list]
    tot = _small_exchange(jnp.concatenate(small_list + [loss_part]), reduce=True, name="reduce_small")
    loss = tot[-1]
    off = 0
    for n, cnt in zip(SMALL + ("c_ctx",), sizes):
        G[n] = tot[off:off + cnt].reshape(P[n].shape if n != "c_ctx" else (D,))
        off += cnt
    sg = jax.nn.sigmoid(A["c_ctx"])
    G["c_ctx"] = G["c_ctx"] * (sg * (1.0 + A["c_ctx"] * (1.0 - sg)))
    fcols = A["ffn_conv_w"].shape[-1]
    G["ffn_conv_w"] = lax.dynamic_slice_in_dim(G["ffn_conv_w"], q * fcols, fcols, axis=2)

    gflat = jnp.concatenate([_to_shards(dP[n], axis) for n, axis in BIG], axis=1)
    gp = jnp.pad(gflat, ((0, 0), (0, 2 * R * PACK_W - n_tot))).reshape(4, 2, R, PACK_W).transpose(1, 0, 2, 3)
    got = _pair_swap(gp)
    pair = _pair_sum(gp, got, c)
    half = _sum4(_chip_exchange(pair), pair, q)
    both = _pair_share(half).reshape(-1)
    off = 0
    for (n, _), cnt in zip(BIG, shard_n):
        G[n] = both[off:off + cnt].reshape(A[n].shape)
        off += cnt

    delta, new_m, new_v = {}, {}, {}
    tiny = [n for n in WEIGHTS if A[n].size < (1 << 18)]
    cat = lambda pre: jnp.concatenate([(A[pre + n] if pre != "g" else G[n]).reshape(-1) for n in tiny])
    n_tiny = sum(A[n].size for n in tiny)
    rt = -(-n_tiny // (8 * PACK_W)) * 8
    shape2 = lambda a: jnp.pad(a, (0, rt * PACK_W - n_tiny)).reshape(rt, PACK_W)
    outs = _adamw_call(shape2(cat("")), shape2(cat("g")), shape2(cat("m_")), shape2(cat("v_")), "adamw_small")
    off = 0
    for n in tiny:
        cnt = A[n].size
        delta[n], new_m[n], new_v[n] = [o.reshape(-1)[off:off + cnt].reshape(A[n].shape) for o in outs]
        off += cnt
    for n in WEIGHTS:
        if n not in tiny:
            delta[n], new_m[n], new_v[n] = _adamw(A[n], G[n], A["m_" + n], A["v_" + n])
    return (loss, grad_x, *[G[n] for n in WEIGHTS], *[delta[n] for n in WEIGHTS],
            *[new_m[n] for n in WEIGHTS], *[new_v[n] for n in WEIGHTS])


def kernel(x, c, ctx, c_ctx, w_mod, b_mod, norm_mix, norm_ffn, mla_w_dq, mla_g_dq, mla_w_uq, mla_g_q_nope, mla_g_q_pe, mla_w_dkv, mla_g_dkv, mla_g_k_pe, mla_w_ukv, mla_g_k_nope, mla_w_o, gqa_w_q, gqa_g_q, gqa_w_kv, gqa_g_k, gqa_w_o, ffn_w_up, ffn_conv_w, ffn_conv_b, ffn_w_down, loss_target, m_c_ctx, m_w_mod, m_b_mod, m_norm_mix, m_norm_ffn, m_mla_w_dq, m_mla_g_dq, m_mla_w_uq, m_mla_g_q_nope, m_mla_g_q_pe, m_mla_w_dkv, m_mla_g_dkv, m_mla_g_k_pe, m_mla_w_ukv, m_mla_g_k_nope, m_mla_w_o, m_gqa_w_q, m_gqa_g_q, m_gqa_w_kv, m_gqa_g_k, m_gqa_w_o, m_ffn_w_up, m_ffn_conv_w, m_ffn_conv_b, m_ffn_w_down, v_c_ctx, v_w_mod, v_b_mod, v_norm_mix, v_norm_ffn, v_mla_w_dq, v_mla_g_dq, v_mla_w_uq, v_mla_g_q_nope, v_mla_g_q_pe, v_mla_w_dkv, v_mla_g_dkv, v_mla_g_k_pe, v_mla_w_ukv, v_mla_g_k_nope, v_mla_w_o, v_gqa_w_q, v_gqa_g_q, v_gqa_w_kv, v_gqa_g_k, v_gqa_w_o, v_ffn_w_up, v_ffn_conv_w, v_ffn_conv_b, v_ffn_w_down):
    return _step(dict(locals()))
```
